```python
import jax, jax.numpy as jnp
from jax import lax
import numpy as np

D_MODEL = 4096
BATCH = 2
SEQ = 4096
DEPTH = 1
DEC_BATCH = 128
DEC_SEQ = 4
PAST_LEN = 2048
PAGE_SIZE = 128

D_RNN = D_MODEL // 2
RNN_BLOCKS = 16
RNN_BW = D_RNN // RNN_BLOCKS
CONV_W = 4
C_GATE = 8.0
HEAD_DIM = 128
N_HEADS = (D_MODEL // 2) // HEAD_DIM
N_KV = 4
GQA_R = N_HEADS // N_KV
D_ATT = N_HEADS * HEAD_DIM
D_KV = N_KV * HEAD_DIM
CMP_BLK = 64
N_SEL = 16
WINDOW = 512
WIN_QBLK = 128
SEL_QCHUNK = 64
SCALE = HEAD_DIM ** -0.5
NEG = -1e30
D_FF = 4 * D_MODEL
ROPE_THETA = 10000.0
EPS = 1e-6
D_IN = 2 * D_RNN + D_ATT + 6 * D_KV + 3 * N_HEADS

kernel_name = 'hymba_rglru_nsa_decoder_step'


def rmsnorm(x, g):
    xf = x.astype(jnp.float32)
    y = xf * lax.rsqrt(jnp.mean(xf * xf, axis=-1, keepdims=True) + EPS)
    return (y * g.astype(jnp.float32)).astype(x.dtype)


def rope(x, pos):
    half = HEAD_DIM // 2
    inv = ROPE_THETA ** (-jnp.arange(half, dtype=jnp.float32) * (2.0 / HEAD_DIM))
    ang = pos.astype(jnp.float32)[:, None] * inv[None, :]
    cos = jnp.cos(ang)[:, None, :]
    sin = jnp.sin(ang)[:, None, :]
    xf = x.astype(jnp.float32)
    x1, x2 = xf[..., :half], xf[..., half:]
    return jnp.concatenate([x1 * cos - x2 * sin, x2 * cos + x1 * sin], axis=-1).astype(x.dtype)


def project(x, pos, g_pre, w_in):
    B, T, _ = x.shape
    u = jnp.einsum('btd,de->bte', rmsnorm(x, g_pre), w_in)
    cuts = np.cumsum([D_RNN, D_RNN, D_ATT, D_KV, D_KV, D_KV, D_KV, D_KV, D_KV]).tolist()
    xr, yr, q, kc, vc, ks, vs, kw, vw, gl = jnp.split(u, cuts, axis=-1)
    heads = lambda a, n: a.reshape(B, T, n, HEAD_DIM)
    q = rope(heads(q, N_HEADS), pos)
    ks = rope(heads(ks, N_KV), pos)
    kw = rope(heads(kw, N_KV), pos)
    gates = jax.nn.sigmoid(gl.astype(jnp.float32)).reshape(B, T, N_HEADS, 3).astype(x.dtype)
    return xr, yr, q, heads(kc, N_KV), heads(vc, N_KV), ks, heads(vs, N_KV), kw, heads(vw, N_KV), gates


def rg_lru_block(xr, yr, conv_buf, h0, pos, conv_w, conv_b, w_ga, b_ga, w_gx, b_gx, lam):
    B, T, _ = xr.shape
    xc = jnp.concatenate([conv_buf.astype(xr.dtype), xr], axis=1)
    xconv = sum(xc[:, k:k + T] * conv_w[k] for k in range(CONV_W)) + conv_b
    new_conv = xc[:, T:]
    xb = xconv.reshape(B, T, RNN_BLOCKS, RNN_BW)
    ga = jax.nn.sigmoid(jnp.einsum('btnc,ncd->btnd', xb, w_ga).reshape(B, T, D_RNN).astype(jnp.float32) + b_ga.astype(jnp.float32))
    gx = jax.nn.sigmoid(jnp.einsum('btnc,ncd->btnd', xb, w_gx).reshape(B, T, D_RNN).astype(jnp.float32) + b_gx.astype(jnp.float32))
    log_a = -C_GATE * ga * jax.nn.softplus(-lam.astype(jnp.float32))
    a = jnp.exp(log_a)
    mult = jnp.sqrt(-jnp.expm1(2.0 * log_a))
    mult = jnp.where((pos == 0)[None, :, None], 1.0, mult)
    bx = mult * gx * xconv.astype(jnp.float32)

    def step(h, ab):
        a_t, b_t = ab
        h = a_t * h + b_t
        return h, h

    hT, hs = lax.scan(step, h0.astype(jnp.float32), (a.swapaxes(0, 1), bx.swapaxes(0, 1)))
    out = hs.swapaxes(0, 1).astype(xr.dtype) * jax.nn.gelu(yr)
    return out, hT.astype(xr.dtype), new_conv


def nsa_cmp_sel(q, q_pos, kc_raw, vc_raw, ks, vs, pe_k, pe_v, w_phi_k, w_phi_v, q_chunk):
    B, Q = q.shape[:2]
    T = kc_raw.shape[1]
    nb = -(-T // CMP_BLK)
    pad = nb * CMP_BLK - T
    blocks = lambda a: jnp.pad(a, ((0, 0), (0, pad), (0, 0), (0, 0))).reshape(B, nb, CMP_BLK, N_KV, HEAD_DIM)
    blk_end = jnp.arange(nb, dtype=jnp.int32) * CMP_BLK + CMP_BLK - 1
    kc = jnp.einsum('bngd,de->bnge', jnp.mean(blocks(kc_raw) + pe_k[:, None, :], axis=2), w_phi_k)
    kc = rope(kc, blk_end)
    vc = jnp.einsum('bngd,de->bnge', jnp.mean(blocks(vc_raw) + pe_v[:, None, :], axis=2), w_phi_v)
    qg = q.reshape(B, Q, N_KV, GQA_R, HEAD_DIM)
    s = jnp.einsum('bqgrd,bngd->bqgrn', qg, kc, preferred_element_type=jnp.float32) * SCALE
    valid = blk_end[None, :] <= q_pos[:, None]
    p = jax.nn.softmax(jnp.where(valid[None, :, None, None, :], s, NEG), axis=-1)
    p = p * jnp.any(valid, axis=-1).astype(jnp.float32)[None, :, None, None, None]
    o_cmp = jnp.einsum('bqgrn,bngd->bqgrd', p.astype(q.dtype), vc).reshape(B, Q, N_HEADS, HEAD_DIM)
    imp = p.sum(axis=3)
    nidx = jnp.arange(nb, dtype=jnp.int32)[None, :]
    cur = (q_pos // CMP_BLK)[:, None]
    forced = (nidx == cur) | (nidx == cur - 1) | (nidx == 0)
    imp = jnp.where(forced[None, :, None, :], 1e4, imp)
    imp = jnp.where((nidx > cur)[None, :, None, :], -1.0, imp)
    _, idx = lax.top_k(imp, min(N_SEL, nb))
    kbs = blocks(ks).transpose(0, 3, 1, 2, 4)
    vbs = blocks(vs).transpose(0, 3, 1, 2, 4)
    nq = Q // q_chunk
    to_chunks = lambda a: a.reshape((B, nq, q_chunk) + a.shape[2:]).swapaxes(0, 1)
    bi = jnp.arange(B)[:, None, None, None]
    gi = jnp.arange(N_KV)[None, None, :, None]
    offs = jnp.arange(CMP_BLK, dtype=jnp.int32)

    def sel_chunk(args):
        qq, ii, pp = args
        kg = kbs[bi, gi, ii]
        vg = vbs[bi, gi, ii]
        ss = jnp.einsum('bcgrd,bcgnld->bcgrnl', qq, kg, preferred_element_type=jnp.float32) * SCALE
        kpos = ii[..., None] * CMP_BLK + offs
        m = kpos <= pp[None, :, None, None, None]
        ss = jnp.where(m[:, :, :, None], ss, NEG)
        pr = jax.nn.softmax(ss.reshape(ss.shape[:4] + (-1,)), axis=-1).reshape(ss.shape)
        return jnp.einsum('bcgrnl,bcgnld->bcgrd', pr.astype(qq.dtype), vg)

    o_sel = lax.map(sel_chunk, (to_chunks(qg), to_chunks(idx), q_pos.reshape(nq, q_chunk)))
    o_sel = o_sel.swapaxes(0, 1).reshape(B, Q, N_HEADS, HEAD_DIM)
    return o_cmp, o_sel


def window_prompt(q, k, v):
    B, S = q.shape[:2]
    nqb = S // WIN_QBLK
    span = WIN_QBLK + WINDOW
    padw = lambda a: jnp.pad(a, ((0, 0), (WINDOW, 0), (0, 0), (0, 0)))
    kidx = jnp.arange(nqb)[:, None] * WIN_QBLK + jnp.arange(span)[None, :]
    kk = padw(k)[:, kidx]
    vv = padw(v)[:, kidx]
    qq = q.reshape(B, nqb, WIN_QBLK, N_KV, GQA_R, HEAD_DIM)
    s = jnp.einsum('bnqgrd,bnkgd->bnqgrk', qq, kk, preferred_element_type=jnp.float32) * SCALE
    qpos = jnp.arange(S).reshape(nqb, WIN_QBLK)[:, :, None]
    kpos = (kidx - WINDOW)[:, None, :]
    m = (kpos <= qpos) & (kpos > qpos - WINDOW) & (kpos >= 0)
    s = jnp.where(m[None, :, :, None, None, :], s, NEG)
    p = jax.nn.softmax(s, axis=-1).astype(q.dtype)
    return jnp.einsum('bnqgrk,bnkgd->bnqgrd', p, vv).reshape(B, S, N_HEADS, HEAD_DIM)


def window_sample(q, q_pos, k_all, v_all, k_pos):
    B, Q = q.shape[:2]
    qg = q.reshape(B, Q, N_KV, GQA_R, HEAD_DIM)
    s = jnp.einsum('bqgrd,bkgd->bqgrk', qg, k_all, preferred_element_type=jnp.float32) * SCALE
    m = (k_pos[None, :] <= q_pos[:, None]) & (k_pos[None, :] > q_pos[:, None] - WINDOW)
    s = jnp.where(m[None, :, None, None, :], s, NEG)
    p = jax.nn.softmax(s, axis=-1).astype(q.dtype)
    return jnp.einsum('bqgrk,bkgd->bqgrd', p, v_all).reshape(B, Q, N_HEADS, HEAD_DIM)


def combine(x, o_rnn, o_cmp, o_sel, o_win, g, g_rnn, g_att, w_out, g_post, g_mlp_pre, w_up, w_down, g_mlp_post):
    B, T, _ = x.shape
    o_att = (g[..., 0:1] * o_cmp + g[..., 1:2] * o_sel + g[..., 2:3] * o_win).reshape(B, T, D_ATT)
    cat = jnp.concatenate([rmsnorm(o_rnn, g_rnn), rmsnorm(o_att, g_att)], axis=-1)
    h = x + rmsnorm(jnp.einsum('bte,ed->btd', cat, w_out), g_post)
    f = jnp.square(jax.nn.relu(jnp.einsum('btd,df->btf', rmsnorm(h, g_mlp_pre), w_up)))
    return h + rmsnorm(jnp.einsum('btf,fd->btd', f, w_down), g_mlp_post)


def setup_inputs(seed: int = 0) -> dict:
    key = jax.random.key(seed)
    ks = iter(jax.random.split(key, 48))
    nrm = lambda shape, s: jax.random.normal(next(ks), shape, jnp.float32) * s
    gain = lambda n: 1.0 + nrm((DEPTH, n), 0.01)
    n_pages = PAST_LEN // PAGE_SIZE
    n_pool = (DEC_BATCH * n_pages * 5) // 4
    lw = min(WINDOW, PAST_LEN)
    pshape = (DEPTH, n_pool, PAGE_SIZE, N_KV, HEAD_DIM)
    wshape = (DEPTH, DEC_BATCH, lw, N_KV, HEAD_DIM)
    page_table = jax.random.permutation(next(ks), n_pool)[:DEC_BATCH * n_pages].reshape(DEC_BATCH, n_pages).astype(jnp.int32)
    a0 = jax.random.uniform(next(ks), (DEPTH, D_RNN), jnp.float32, 0.9, 0.999)
    sg = a0 ** (1.0 / C_GATE)
    lam = jnp.log(sg) - jnp.log1p(-sg)
    return {
        'x_prompt': nrm((BATCH, SEQ, D_MODEL), 1.0),
        'x_sample': nrm((DEC_BATCH, DEC_SEQ, D_MODEL), 1.0),
        'cache_k_cmp': nrm(pshape, 1.0),
        'cache_v_cmp': nrm(pshape, 1.0),
        'cache_k_sel': nrm(pshape, 1.0),
        'cache_v_sel': nrm(pshape, 1.0),
        'cache_k_win': nrm(wshape, 1.0),
        'cache_v_win': nrm(wshape, 1.0),
        'state_h': nrm((DEPTH, DEC_BATCH, D_RNN), 0.5),
        'state_conv': nrm((DEPTH, DEC_BATCH, CONV_W - 1, D_RNN), 1.0),
        'page_table': page_table,
        'norm_mix_pre': gain(D_MODEL),
        'w_in': nrm((DEPTH, D_MODEL, D_IN), D_MODEL ** -0.5),
        'conv_w': nrm((DEPTH, CONV_W, D_RNN), CONV_W ** -0.5),
        'conv_b': nrm((DEPTH, D_RNN), 0.01),
        'w_ga': nrm((DEPTH, RNN_BLOCKS, RNN_BW, RNN_BW), RNN_BW ** -0.5),
        'b_ga': nrm((DEPTH, D_RNN), 0.01),
        'w_gx': nrm((DEPTH, RNN_BLOCKS, RNN_BW, RNN_BW), RNN_BW ** -0.5),
        'b_gx': nrm((DEPTH, D_RNN), 0.01),
        'lam': lam,
        'pe_k': nrm((DEPTH, CMP_BLK, HEAD_DIM), 0.02),
        'pe_v': nrm((DEPTH, CMP_BLK, HEAD_DIM), 0.02),
        'w_phi_k': nrm((DEPTH, HEAD_DIM, HEAD_DIM), HEAD_DIM ** -0.5),
        'w_phi_v': nrm((DEPTH, HEAD_DIM, HEAD_DIM), HEAD_DIM ** -0.5),
        'norm_rnn_out': gain(D_RNN),
        'norm_att_out': gain(D_ATT),
        'w_out': nrm((DEPTH, D_RNN + D_ATT, D_MODEL), (D_RNN + D_ATT) ** -0.5),
        'norm_mix_post': gain(D_MODEL),
        'norm_mlp_pre': gain(D_MODEL),
        'w_up': nrm((DEPTH, D_MODEL, D_FF), D_MODEL ** -0.5),
        'w_down': nrm((DEPTH, D_FF, D_MODEL), D_FF ** -0.5),
        'norm_mlp_post': gain(D_MODEL),
    }


def reference(x_prompt, x_sample, cache_k_cmp, cache_v_cmp, cache_k_sel, cache_v_sel, cache_k_win, cache_v_win,
              state_h, state_conv, page_table, norm_mix_pre, w_in, conv_w, conv_b, w_ga, b_ga, w_gx, b_gx, lam,
              pe_k, pe_v, w_phi_k, w_phi_v, norm_rnn_out, norm_att_out, w_out, norm_mix_post, norm_mlp_pre,
              w_up, w_down, norm_mlp_post):
    xp, xs = x_prompt, x_sample
    Bp, S, _ = xp.shape
    Bs, Q, _ = xs.shape
    pos_p = jnp.arange(S, dtype=jnp.int32)
    pos_s = PAST_LEN + jnp.arange(Q, dtype=jnp.int32)
    lw_in = cache_k_win.shape[2]
    lw_p = min(WINDOW, S)
    lw_s = min(WINDOW, PAST_LEN + Q)
    win_pos = jnp.concatenate([PAST_LEN - lw_in + jnp.arange(lw_in, dtype=jnp.int32), pos_s])
    names = ['pkc', 'pvc', 'pks', 'pvs', 'pkw', 'pvw', 'ph', 'pconv',
             'skc', 'svc', 'sks', 'svs', 'skw', 'svw', 'sh', 'sconv']
    st = {n: [] for n in names}
    for l in range(DEPTH):
        rnn_p = (conv_w[l], conv_b[l], w_ga[l], b_ga[l], w_gx[l], b_gx[l], lam[l])
        cmp_p = (pe_k[l], pe_v[l], w_phi_k[l], w_phi_v[l])
        out_p = (norm_rnn_out[l], norm_att_out[l], w_out[l], norm_mix_post[l], norm_mlp_pre[l],
                 w_up[l], w_down[l], norm_mlp_post[l])
        xr, yr, q, kc, vc, ksl, vsl, kw, vw, g = project(xp, pos_p, norm_mix_pre[l], w_in[l])
        o_rnn, h_p, cv_p = rg_lru_block(xr, yr, jnp.zeros((Bp, CONV_W - 1, D_RNN), xp.dtype),
                                        jnp.zeros((Bp, D_RNN), xp.dtype), pos_p, *rnn_p)
        o_cmp, o_sel = nsa_cmp_sel(q, pos_p, kc, vc, ksl, vsl, *cmp_p, SEL_QCHUNK)
        o_win = window_prompt(q, kw, vw)
        xp = combine(xp, o_rnn, o_cmp, o_sel, o_win, g, *out_p)
        for n, a in zip(names[:8], (kc, vc, ksl, vsl, kw[:, S - lw_p:], vw[:, S - lw_p:], h_p, cv_p)):
            st[n].append(a)
        xr, yr, q, kc, vc, ksl, vsl, kw, vw, g = project(xs, pos_s, norm_mix_pre[l], w_in[l])
        o_rnn, h_s, cv_s = rg_lru_block(xr, yr, state_conv[l], state_h[l], pos_s, *rnn_p)
        paged = lambda c, new: jnp.concatenate(
            [c[l][page_table].reshape(Bs, PAST_LEN, N_KV, HEAD_DIM).astype(new.dtype), new], axis=1)
        o_cmp, o_sel = nsa_cmp_sel(q, pos_s, paged(cache_k_cmp, kc), paged(cache_v_cmp, vc),
                                   paged(cache_k_sel, ksl), paged(cache_v_sel, vsl), *cmp_p, 1)
        kw_all = jnp.concatenate([cache_k_win[l].astype(kw.dtype), kw], axis=1)
        vw_all = jnp.concatenate([cache_v_win[l].astype(vw.dtype), vw], axis=1)
        o_win = window_sample(q, pos_s, kw_all, vw_all, win_pos)
        xs = combine(xs, o_rnn, o_cmp, o_sel, o_win, g, *out_p)
        nk = kw_all.shape[1]
        for n, a in zip(names[8:], (kc, vc, ksl, vsl, kw_all[:, nk - lw_s:], vw_all[:, nk - lw_s:], h_s, cv_s)):
            st[n].append(a)
    return (xp, xs,
            jnp.stack(st['pkc']), jnp.stack(st['pvc']), jnp.stack(st['pks']), jnp.stack(st['pvs']),
            jnp.stack(st['pkw']), jnp.stack(st['pvw']), jnp.stack(st['ph']), jnp.stack(st['pconv']),
            jnp.stack(st['skc']), jnp.stack(st['svc']), jnp.stack(st['sks']), jnp.stack(st['svs']),
            jnp.stack(st['skw']), jnp.stack(st['svw']), jnp.stack(st['sh']), jnp.stack(st['sconv']))
```

```python
import functools

import jax
import jax.numpy as jnp
import numpy as np
from jax import lax
from jax.experimental import pallas as pl
from jax.experimental.pallas import tpu as pltpu

F32 = jnp.float32
BF16 = jnp.bfloat16

HEAD_DIM = 128
N_KV = 4
GQA_R = 4
N_HEADS = N_KV * GQA_R
D_ATT = N_HEADS * HEAD_DIM
D_KV = N_KV * HEAD_DIM
CMP_BLK = 64
N_SEL = 16
WINDOW = 512
CONV_W = 4
C_GATE = 8.0
RNN_BLOCKS = 16
RNN_BW = 128
ROPE_THETA = 10000.0
EPS = 1e-6
NEG = -1e30
PAGE_SIZE = 128
SCALE = HEAD_DIM ** -0.5
LANES = 128
VMEM_LIMIT = 56 * 1024 * 1024


def _cparams(sem):
    return pltpu.CompilerParams(dimension_semantics=sem, vmem_limit_bytes=VMEM_LIMIT)


def _rope_tables(pos):
    half = HEAD_DIM // 2
    inv = ROPE_THETA ** (-jnp.arange(half, dtype=F32) * (2.0 / HEAD_DIM))
    ang = pos.astype(F32)[:, None] * inv[None, :]
    c, s = jnp.cos(ang), jnp.sin(ang)
    return jnp.concatenate([c, c], axis=-1), jnp.concatenate([-s, s], axis=-1)


def _rope(x, cos2, sin2):
    return x * cos2 + pltpu.roll(x, HEAD_DIM // 2, 1) * sin2


def _rms(x):
    return x * lax.rsqrt(jnp.mean(x * x, axis=-1, keepdims=True) + EPS)


def _gelu(x):
    return 0.5 * x * (1.0 + jnp.tanh(0.7978845608028654 * (x + 0.044715 * (x * x * x))))


def _softplus(x):
    return jnp.maximum(x, 0.0) + jnp.log1p(jnp.exp(-jnp.abs(x)))


def _sigmoid(x):
    return 1.0 / (1.0 + jnp.exp(-x))


def _norm_kernel(x_ref, g_ref, o_ref):
    o_ref[...] = (_rms(x_ref[...]) * g_ref[...]).astype(o_ref.dtype)


def _norm_cast(x, g, tm):
    m, d = x.shape
    return pl.pallas_call(
        _norm_kernel,
        grid=(m // tm,),
        in_specs=[pl.BlockSpec((tm, d), lambda i: (i, 0)), pl.BlockSpec((1, d), lambda i: (0, 0))],
        out_specs=pl.BlockSpec((tm, d), lambda i: (i, 0)),
        out_shape=jax.ShapeDtypeStruct((m, d), BF16),
        compiler_params=_cparams(("parallel",)),
        name="norm_cast",
    )(x, g)


def _post_kernel(acc_ref, res_ref, g_ref, g2_ref, y_ref, *hn_ref):
    y = res_ref[...] + _rms(acc_ref[...]) * g_ref[...]
    y_ref[...] = y
    if hn_ref:
        hn_ref[0][...] = (_rms(y) * g2_ref[...]).astype(BF16)


def _post_norm(acc, res, g, g2, tm, with_hn):
    m, d = acc.shape
    row = pl.BlockSpec((tm, d), lambda i: (i, 0))
    vec = pl.BlockSpec((1, d), lambda i: (0, 0))
    out_shape = [jax.ShapeDtypeStruct((m, d), F32)]
    out_specs = [row]
    if with_hn:
        out_shape.append(jax.ShapeDtypeStruct((m, d), BF16))
        out_specs.append(row)
    return pl.pallas_call(
        _post_kernel,
        grid=(m // tm,),
        in_specs=[row, row, vec, vec],
        out_specs=out_specs,
        out_shape=out_shape,
        compiler_params=_cparams(("parallel",)),
        name="post_norm",
    )(acc, res, g, g2)


def _cat_norm_kernel(orn_ref, o3_ref, gate_ref, grn_ref, gat_ref, cat_ref, tmp):
    d = orn_ref.shape[1]
    cat_ref[:, :d] = (_rms(orn_ref[...].astype(F32)) * grn_ref[...]).astype(BF16)
    g = gate_ref[...]
    ss = jnp.zeros((g.shape[0], 1), F32)
    for h in range(N_HEADS):
        sl = slice(h * HEAD_DIM, (h + 1) * HEAD_DIM)
        oh = (g[:, 3 * h:3 * h + 1] * o3_ref[0, :, sl].astype(F32)
              + g[:, 3 * h + 1:3 * h + 2] * o3_ref[1, :, sl].astype(F32)
              + g[:, 3 * h + 2:3 * h + 3] * o3_ref[2, :, sl].astype(F32))
        tmp[:, sl] = oh
        ss = ss + jnp.sum(oh * oh, axis=-1, keepdims=True)
    inv = lax.rsqrt(ss * (1.0 / D_ATT) + EPS)
    cat_ref[:, d:] = (tmp[...] * inv * gat_ref[...]).astype(BF16)


def _cat_norm(o_rnn, o3, gates, g_rnn, g_att, tm):
    m, d = o_rnn.shape
    return pl.pallas_call(
        _cat_norm_kernel,
        grid=(m // tm,),
        in_specs=[pl.BlockSpec((tm, d), lambda i: (i, 0)),
                  pl.BlockSpec((3, tm, D_ATT), lambda i: (0, i, 0)),
                  pl.BlockSpec((tm, LANES), lambda i: (i, 0)),
                  pl.BlockSpec((1, d), lambda i: (0, 0)),
                  pl.BlockSpec((1, D_ATT), lambda i: (0, 0))],
        out_specs=pl.BlockSpec((tm, d + D_ATT), lambda i: (i, 0)),
        out_shape=jax.ShapeDtypeStruct((m, d + D_ATT), BF16),
        scratch_shapes=[pltpu.VMEM((tm, D_ATT), F32)],
        compiler_params=_cparams(("parallel",)),
        name="cat_norm",
    )(o_rnn, o3, gates, g_rnn, g_att)


def _dot(a, b):
    return jnp.dot(a, b, preferred_element_type=F32)


def _dot_nt(a, b):
    return lax.dot_general(a, b, (((1,), (1,)), ((), ())), preferred_element_type=F32)


def _mm_plain_kernel(a_ref, b_ref, o_ref):
    o_ref[...] = _dot(a_ref[...], b_ref[...])


def _mm_gate_kernel(a_ref, b_ref, o_ref):
    o_ref[...] = _sigmoid(_dot(a_ref[...], b_ref[...]))


def _matmul(a, b, tm, tn, body=_mm_plain_kernel, name="matmul"):
    m, k = a.shape
    n = b.shape[1]
    return pl.pallas_call(
        body,
        grid=(m // tm, n // tn),
        in_specs=[pl.BlockSpec((tm, k), lambda i, j: (i, 0)), pl.BlockSpec((k, tn), lambda i, j: (0, j))],
        out_specs=pl.BlockSpec((tm, tn), lambda i, j: (i, j)),
        out_shape=jax.ShapeDtypeStruct((m, n), F32),
        compiler_params=_cparams(("parallel", "arbitrary")),
        name=name,
    )(a, b)


def _mm_q_prompt_kernel(a_ref, b_ref, cos_ref, sin_ref, o_ref):
    acc = _dot(a_ref[...], b_ref[...])
    cos2, sin2 = cos_ref[...], sin_ref[...]
    for r in range(GQA_R):
        x = acc[:, r * HEAD_DIM:(r + 1) * HEAD_DIM]
        o_ref[r] = (_rope(x, cos2, sin2) * SCALE).astype(BF16)


def _proj_q_prompt(a, w, cos2, sin2, tm):
    m, k = a.shape
    nt = cos2.shape[0] // tm
    return pl.pallas_call(
        _mm_q_prompt_kernel,
        grid=(m // tm, N_KV),
        in_specs=[pl.BlockSpec((tm, k), lambda i, j: (i, 0)),
                  pl.BlockSpec((k, GQA_R * HEAD_DIM), lambda i, j: (0, j)),
                  pl.BlockSpec((tm, HEAD_DIM), lambda i, j: (i % nt, 0)),
                  pl.BlockSpec((tm, HEAD_DIM), lambda i, j: (i % nt, 0))],
        out_specs=pl.BlockSpec((GQA_R, tm, HEAD_DIM), lambda i, j: (j, i, 0)),
        out_shape=jax.ShapeDtypeStruct((N_HEADS, m, HEAD_DIM), BF16),
        compiler_params=_cparams(("parallel", "arbitrary")),
        name="proj_q_prompt",
    )(a, w, cos2, sin2)


def _mm_q_sample_kernel(a_ref, b_ref, cos_ref, sin_ref, o_ref, tmp, *, n_tok):
    acc = _dot(a_ref[...], b_ref[...])
    cos2, sin2 = cos_ref[...], sin_ref[...]
    nb = o_ref.shape[1]
    for r in range(GQA_R):
        x = acc[:, r * HEAD_DIM:(r + 1) * HEAD_DIM]
        tmp[...] = _rope(x, cos2, sin2) * SCALE
        for t in range(n_tok):
            o_ref[0, :, n_tok * r + t, :] = tmp[t * nb:(t + 1) * nb, :]


def _proj_q_sample(a, w, cos2, sin2, n_tok):
    m, k = a.shape
    nb = m // n_tok
    return pl.pallas_call(
        functools.partial(_mm_q_sample_kernel, n_tok=n_tok),
        grid=(N_KV,),
        in_specs=[pl.BlockSpec((m, k), lambda j: (0, 0)),
                  pl.BlockSpec((k, GQA_R * HEAD_DIM), lambda j: (0, j)),
                  pl.BlockSpec((m, HEAD_DIM), lambda j: (0, 0)),
                  pl.BlockSpec((m, HEAD_DIM), lambda j: (0, 0))],
        out_specs=pl.BlockSpec((1, nb, GQA_R * n_tok, HEAD_DIM), lambda j: (j, 0, 0, 0)),
        out_shape=jax.ShapeDtypeStruct((N_KV, nb, GQA_R * n_tok, HEAD_DIM), F32),
        scratch_shapes=[pltpu.VMEM((m, HEAD_DIM), F32)],
        compiler_params=_cparams(("arbitrary",)),
        name="proj_q_sample",
    )(a, w, cos2, sin2)


def _mm_kv_kernel(a_ref, b_ref, cos_ref, sin_ref, *outs, with_hm):
    j = pl.program_id(1)
    acc = _dot(a_ref[...], b_ref[...])
    is_rope = jnp.logical_or(j == 2, j == 4)
    cos2 = jnp.where(is_rope, cos_ref[...], 1.0)
    sin2 = jnp.where(is_rope, sin_ref[...], 0.0)
    ys = [_rope(acc[:, g * HEAD_DIM:(g + 1) * HEAD_DIM], cos2, sin2) for g in range(N_KV)]
    for s in range(6):
        @pl.when(j == s)
        def _(s=s):
            for g in range(N_KV):
                outs[s][:, g, :] = ys[g]
    if with_hm:
        for g in range(N_KV):
            outs[6][0, g] = ys[g].astype(BF16)


def _proj_kv(a, w, cos2, sin2, tm, with_hm):
    m, k = a.shape
    nt = cos2.shape[0] // tm
    nat = pl.BlockSpec((tm, N_KV, HEAD_DIM), lambda i, j: (i, 0, 0))
    out_specs = [nat] * 6
    out_shape = [jax.ShapeDtypeStruct((m, N_KV, HEAD_DIM), F32)] * 6
    if with_hm:
        out_specs = out_specs + [pl.BlockSpec((1, N_KV, tm, HEAD_DIM), lambda i, j: (j, 0, i, 0))]
        out_shape = out_shape + [jax.ShapeDtypeStruct((6, N_KV, m, HEAD_DIM), BF16)]
    return pl.pallas_call(
        functools.partial(_mm_kv_kernel, with_hm=with_hm),
        grid=(m // tm, 6),
        in_specs=[pl.BlockSpec((tm, k), lambda i, j: (i, 0)),
                  pl.BlockSpec((k, D_KV), lambda i, j: (0, j)),
                  pl.BlockSpec((tm, HEAD_DIM), lambda i, j: (i % nt, 0)),
                  pl.BlockSpec((tm, HEAD_DIM), lambda i, j: (i % nt, 0))],
        out_specs=out_specs,
        out_shape=out_shape,
        compiler_params=_cparams(("parallel", "arbitrary")),
        name="proj_kv",
    )(a, w, cos2, sin2)


def _mlp_kernel(h_ref, wu_ref, wd_ref, o_ref):
    j = pl.program_id(1)
    u = _dot(h_ref[...], wu_ref[...])
    f = jnp.square(jnp.maximum(u, 0.0)).astype(BF16)
    d = _dot(f, wd_ref[...])

    @pl.when(j == 0)
    def _():
        o_ref[...] = d

    @pl.when(j > 0)
    def _():
        o_ref[...] += d


def _mlp(hn, w_up, w_down, tm, tf):
    m, d = hn.shape
    f = w_up.shape[1]
    return pl.pallas_call(
        _mlp_kernel,
        grid=(m // tm, f // tf),
        in_specs=[pl.BlockSpec((tm, d), lambda i, j: (i, 0)),
                  pl.BlockSpec((d, tf), lambda i, j: (0, j)),
                  pl.BlockSpec((tf, d), lambda i, j: (j, 0))],
        out_specs=pl.BlockSpec((tm, d), lambda i, j: (i, 0)),
        out_shape=jax.ShapeDtypeStruct((m, d), F32),
        compiler_params=_cparams(("parallel", "arbitrary")),
        name="mlp",
    )(hn, w_up, w_down)


def _rglru_gates(xc, gg, bga, bgx, lam, first_row):
    ra = _sigmoid(gg[:, :RNN_BW] + bga)
    rx = _sigmoid(gg[:, RNN_BW:] + bgx)
    log_a = (-C_GATE) * ra * _softplus(-lam)
    a = jnp.exp(log_a)
    th = jnp.tanh(log_a)
    mult = jnp.sqrt((-2.0) * th / (1.0 - th))
    if first_row is not None:
        mult = jnp.where(first_row, 1.0, mult)
    return a, mult * rx * xc


def _rglru_prompt_kernel(xr_ref, yr_ref, cw_ref, cb_ref, wg_ref, bga_ref, bgx_ref, lam_ref,
                         o_ref, h_ref, conv_ref, xbuf, hcar):
    ti = pl.program_id(1)
    tt = xr_ref.shape[0]

    @pl.when(ti == 0)
    def _():
        xbuf[0:8, :] = jnp.zeros((8, xbuf.shape[1]), F32)
        hcar[...] = jnp.zeros(hcar.shape, F32)

    xbuf[8:8 + tt, :] = xr_ref[...]
    row = lax.broadcasted_iota(jnp.int32, (tt, RNN_BW), 0)
    first_row = jnp.logical_and(row == 0, ti == 0)

    def block(n, carry):
        col = pl.ds(pl.multiple_of(n * RNN_BW, RNN_BW), RNN_BW)
        xc = cb_ref[:, col]
        for k in range(CONV_W):
            xc = xc + xbuf[8 - (CONV_W - 1) + k:8 - (CONV_W - 1) + k + tt, col] * cw_ref[k:k + 1, col]
        gg = _dot(xc.astype(BF16), wg_ref[n])
        a, b = _rglru_gates(xc, gg, bga_ref[:, col], bgx_ref[:, col], lam_ref[:, col], first_row)
        s = 1
        while s < tt:
            keep = row >= s
            a_sh = jnp.where(keep, pltpu.roll(a, s, 0), 1.0)
            b_sh = jnp.where(keep, pltpu.roll(b, s, 0), 0.0)
            b = a * b_sh + b
            a = a * a_sh
            s *= 2
        hs = a * hcar[:, col] + b
        hcar[:, col] = hs[tt - 1:tt, :]
        o_ref[:, col] = (hs * _gelu(yr_ref[:, col])).astype(o_ref.dtype)
        return carry

    lax.fori_loop(0, RNN_BLOCKS, block, 0)
    xbuf[0:8, :] = xbuf[tt:tt + 8, :]

    @pl.when(ti == pl.num_programs(1) - 1)
    def _():
        h_ref[0] = hcar[...]
        conv_ref[0] = xbuf[8 - (CONV_W - 1):8, :]


def _rglru_prompt(xy, conv_w, conv_b, wg, b_ga, b_gx, lam, nbatch, seq, tt):
    d = conv_w.shape[1]
    nt = seq // tt
    vec = pl.BlockSpec((1, d), lambda b, t: (0, 0))
    return pl.pallas_call(
        _rglru_prompt_kernel,
        grid=(nbatch, nt),
        in_specs=[pl.BlockSpec((tt, d), lambda b, t: (b * nt + t, 0)),
                  pl.BlockSpec((tt, d), lambda b, t: (b * nt + t, 1)),
                  pl.BlockSpec((CONV_W, d), lambda b, t: (0, 0)),
                  vec,
                  pl.BlockSpec((RNN_BLOCKS, RNN_BW, 2 * RNN_BW), lambda b, t: (0, 0, 0)),
                  vec, vec, vec],
        out_specs=[pl.BlockSpec((tt, d), lambda b, t: (b * nt + t, 0)),
                   pl.BlockSpec((1, 1, d), lambda b, t: (b, 0, 0)),
                   pl.BlockSpec((1, CONV_W - 1, d), lambda b, t: (b, 0, 0))],
        out_shape=[jax.ShapeDtypeStruct((nbatch * seq, d), BF16),
                   jax.ShapeDtypeStruct((nbatch, 1, d), F32),
                   jax.ShapeDtypeStruct((nbatch, CONV_W - 1, d), F32)],
        scratch_shapes=[pltpu.VMEM((tt + 8, d), F32), pltpu.VMEM((1, d), F32)],
        compiler_params=_cparams(("arbitrary", "arbitrary")),
        name="rglru_prompt",
    )(xy, xy, conv_w, conv_b, wg, b_ga, b_gx, lam)


def _rglru_sample_kernel(xr_ref, yr_ref, cst_ref, h0_ref, cw_ref, cb_ref, wg_ref, bga_ref, bgx_ref, lam_ref,
                         o_ref, h_ref, conv_ref, *, n_tok, pos0):
    nb = h0_ref.shape[0]
    xs = [cst_ref[k] for k in range(CONV_W - 1)] + [xr_ref[t * nb:(t + 1) * nb, :] for t in range(n_tok)]
    h = h0_ref[...]
    for t in range(n_tok):
        xc = cb_ref[...]
        for k in range(CONV_W):
            xc = xc + xs[t + k] * cw_ref[k:k + 1, :]
        gg = _dot(xc.astype(BF16), wg_ref[0])
        a, b = _rglru_gates(xc, gg, bga_ref[...], bgx_ref[...], lam_ref[...], True if pos0 + t == 0 else None)
        h = a * h + b
        o_ref[:, t, :] = h * _gelu(yr_ref[t * nb:(t + 1) * nb, :])
    h_ref[...] = h
    for k in range(CONV_W - 1):
        conv_ref[k] = xs[n_tok + k]


def _rglru_sample(xy, cst, h0, conv_w, conv_b, wg, b_ga, b_gx, lam, n_tok, pos0):
    nb, d = h0.shape
    m = nb * n_tok
    vec = pl.BlockSpec((1, RNN_BW), lambda n: (0, n))
    st = pl.BlockSpec((CONV_W - 1, nb, RNN_BW), lambda n: (0, 0, n))
    return pl.pallas_call(
        functools.partial(_rglru_sample_kernel, n_tok=n_tok, pos0=pos0),
        grid=(RNN_BLOCKS,),
        in_specs=[pl.BlockSpec((m, RNN_BW), lambda n: (0, n)),
                  pl.BlockSpec((m, RNN_BW), lambda n: (0, RNN_BLOCKS + n)),
                  st,
                  pl.BlockSpec((nb, RNN_BW), lambda n: (0, n)),
                  pl.BlockSpec((CONV_W, RNN_BW), lambda n: (0, n)),
                  vec,
                  pl.BlockSpec((1, RNN_BW, 2 * RNN_BW), lambda n: (n, 0, 0)),
                  vec, vec, vec],
        out_specs=[pl.BlockSpec((nb, n_tok, RNN_BW), lambda n: (0, 0, n)),
                   pl.BlockSpec((nb, RNN_BW), lambda n: (0, n)),
                   st],
        out_shape=[jax.ShapeDtypeStruct((nb, n_tok, d), F32),
                   jax.ShapeDtypeStruct((nb, d), F32),
                   jax.ShapeDtypeStruct((CONV_W - 1, nb, d), F32)],
        compiler_params=_cparams(("parallel",)),
        name="rglru_sample",
    )(xy, xy, cst, h0, conv_w, conv_b, wg, b_ga, b_gx, lam)


def _block_summary(x, pe_ref, w_ref, nblk):
    m = jnp.sum(x.reshape(nblk, CMP_BLK, HEAD_DIM), axis=1) * (1.0 / CMP_BLK)
    m = m + jnp.mean(pe_ref[...], axis=0, keepdims=True)
    return _dot(m.astype(BF16), w_ref[...])


def _cmp_prompt_kernel(kc_ref, vc_ref, pek_ref, pev_ref, wk_ref, wv_ref, cos_ref, sin_ref, ko_ref, vo_ref):
    nblk = ko_ref.shape[2]
    for g in range(N_KV):
        kk = _block_summary(kc_ref[:, g, :], pek_ref, wk_ref, nblk)
        ko_ref[0, g] = _rope(kk, cos_ref[...], sin_ref[...]).astype(BF16)
        vo_ref[0, g] = _block_summary(vc_ref[:, g, :], pev_ref, wv_ref, nblk).astype(BF16)


def _cmp_prompt(kc, vc, pe_k, pe_v, w_k, w_v, cos_e, sin_e, nbatch, seq, rows):
    nblk = rows // CMP_BLK
    nt = seq // rows
    nat = pl.BlockSpec((rows, N_KV, HEAD_DIM), lambda b, t: (b * nt + t, 0, 0))
    full = lambda shape: pl.BlockSpec(shape, lambda b, t: (0,) * len(shape))
    tab = pl.BlockSpec((nblk, HEAD_DIM), lambda b, t: (t, 0))
    out = pl.BlockSpec((1, N_KV, nblk, HEAD_DIM), lambda b, t: (b, 0, t, 0))
    shape = jax.ShapeDtypeStruct((nbatch, N_KV, seq // CMP_BLK, HEAD_DIM), BF16)
    return pl.pallas_call(
        _cmp_prompt_kernel,
        grid=(nbatch, nt),
        in_specs=[nat, nat, full((CMP_BLK, HEAD_DIM)), full((CMP_BLK, HEAD_DIM)),
                  full((HEAD_DIM, HEAD_DIM)), full((HEAD_DIM, HEAD_DIM)), tab, tab],
        out_specs=[out, out],
        out_shape=[shape, shape],
        compiler_params=_cparams(("parallel", "arbitrary")),
        name="cmp_prompt",
    )(kc, vc, pe_k, pe_v, w_k, w_v, cos_e, sin_e)


def _page_copies(pt_ref, hbm, buf, sem, bb, slot, n_pages):
    return [pltpu.make_async_copy(hbm.at[0, pt_ref[bb, p]], buf.at[slot, pl.ds(p * PAGE_SIZE, PAGE_SIZE)], sem.at[slot])
            for p in range(n_pages)]


def _gather_step(pt_ref, hbms, bufs, sems, n_pages):
    b = pl.program_id(0)
    slot = b % 2

    @pl.when(b == 0)
    def _():
        for hbm, buf, sem in zip(hbms, bufs, sems):
            for c in _page_copies(pt_ref, hbm, buf, sem, 0, 0, n_pages):
                c.start()

    @pl.when(b + 1 < pl.num_programs(0))
    def _():
        for hbm, buf, sem in zip(hbms, bufs, sems):
            for c in _page_copies(pt_ref, hbm, buf, sem, b + 1, 1 - slot, n_pages):
                c.start()

    for hbm, buf, sem in zip(hbms, bufs, sems):
        for c in _page_copies(pt_ref, hbm, buf, sem, b, slot, n_pages):
            c.wait()
    return slot


def _cmp_sample_kernel(pt_ref, kc_hbm, vc_hbm, kn_ref, vn_ref, pek_ref, pev_ref, wk_ref, wv_ref, cos_ref, sin_ref,
                       ko_ref, vo_ref, kbuf, vbuf, ksem, vsem, *, n_pages):
    slot = _gather_step(pt_ref, (kc_hbm, vc_hbm), (kbuf, vbuf), (ksem, vsem), n_pages)
    nblk = n_pages * PAGE_SIZE // CMP_BLK
    npad = ko_ref.shape[2] - nblk

    def tail(new_ref, pe_ref, w_ref, g):
        m = (jnp.sum(new_ref[:, 0, g, :], axis=0, keepdims=True) + jnp.sum(pe_ref[...], axis=0, keepdims=True)) * (1.0 / CMP_BLK)
        return _dot(jnp.broadcast_to(m, (npad, HEAD_DIM)).astype(BF16), w_ref[...])

    for g in range(N_KV):
        kk = _block_summary(kbuf[slot, :, g, :], pek_ref, wk_ref, nblk)
        ko_ref[0, g, 0:nblk, :] = _rope(kk, cos_ref[0:nblk, :], sin_ref[0:nblk, :])
        kt = tail(kn_ref, pek_ref, wk_ref, g)
        ko_ref[0, g, nblk:nblk + npad, :] = _rope(kt, cos_ref[nblk:nblk + npad, :], sin_ref[nblk:nblk + npad, :])
        vo_ref[0, g, 0:nblk, :] = _block_summary(vbuf[slot, :, g, :], pev_ref, wv_ref, nblk)
        vo_ref[0, g, nblk:nblk + npad, :] = tail(vn_ref, pev_ref, wv_ref, g)


def _cmp_sample(page_table, cache_k, cache_v, k_new, v_new, pe_k, pe_v, w_k, w_v, cos_e, sin_e, npad):
    nb, n_pages = page_table.shape
    past = n_pages * PAGE_SIZE
    nblk = past // CMP_BLK
    n_tok = k_new.shape[0]
    full = lambda shape: pl.BlockSpec(shape, lambda b, pt: (0,) * len(shape))
    new = pl.BlockSpec((n_tok, 1, N_KV, HEAD_DIM), lambda b, pt: (0, b, 0, 0))
    out = pl.BlockSpec((1, N_KV, nblk + npad, HEAD_DIM), lambda b, pt: (b, 0, 0, 0))
    shape = jax.ShapeDtypeStruct((nb, N_KV, nblk + npad, HEAD_DIM), F32)
    return pl.pallas_call(
        functools.partial(_cmp_sample_kernel, n_pages=n_pages),
        grid_spec=pltpu.PrefetchScalarGridSpec(
            num_scalar_prefetch=1,
            grid=(nb,),
            in_specs=[pl.BlockSpec(memory_space=pl.ANY), pl.BlockSpec(memory_space=pl.ANY), new, new,
                      full((CMP_BLK, HEAD_DIM)), full((CMP_BLK, HEAD_DIM)),
                      full((HEAD_DIM, HEAD_DIM)), full((HEAD_DIM, HEAD_DIM)),
                      full((nblk + npad, HEAD_DIM)), full((nblk + npad, HEAD_DIM))],
            out_specs=[out, out],
            scratch_shapes=[pltpu.VMEM((2, past, N_KV, HEAD_DIM), F32), pltpu.VMEM((2, past, N_KV, HEAD_DIM), F32),
                            pltpu.SemaphoreType.DMA((2,)), pltpu.SemaphoreType.DMA((2,))]),
        out_shape=[shape, shape],
        compiler_params=_cparams(("arbitrary",)),
        name="cmp_sample",
    )(page_table, cache_k, cache_v, k_new, v_new, pe_k, pe_v, w_k, w_v, cos_e, sin_e)


def _select_blocks_t(imp_t, cur, nsel):
    nblk = imp_t.shape[0]
    nidx = lax.broadcasted_iota(jnp.int32, imp_t.shape, 0)
    forced = jnp.logical_or(jnp.logical_or(nidx == cur, nidx == cur - 1), nidx == 0)
    imp_t = jnp.where(forced, 1e4, imp_t)
    imp_t = jnp.where(nidx > cur, -1.0, imp_t)
    rank = jnp.zeros(imp_t.shape, F32)
    for i in range(nblk):
        row = imp_t[i:i + 1, :]
        rank = rank + jnp.where(nidx > i, jnp.where(row >= imp_t, 1.0, 0.0), jnp.where(row > imp_t, 1.0, 0.0))
    return jnp.where(rank < float(nsel), 1.0, 0.0)


def _softmax_lanes(s):
    m = jnp.max(s, axis=-1, keepdims=True)
    e = jnp.exp(s - m)
    return e / jnp.sum(e, axis=-1, keepdims=True)


def _attn_prompt_kernel(q_ref, kc_ref, vc_ref, ks_ref, vs_ref, kw_ref, vw_ref, e_ref, o_ref, *, kt, nsel):
    qi = pl.program_id(2)
    tq = q_ref.shape[1]
    rows = GQA_R * tq
    nblk = kc_ref.shape[2]
    q4 = q_ref[...].reshape(rows, HEAD_DIM)
    qpos = qi * tq + lax.broadcasted_iota(jnp.int32, (tq, 1), 0)

    kc = kc_ref[0, 0]
    blk_end = lax.broadcasted_iota(jnp.int32, (1, nblk), 1) * CMP_BLK + (CMP_BLK - 1)
    valid = blk_end <= qpos
    s = _dot_nt(q4, kc).reshape(GQA_R, tq, nblk)
    p = _softmax_lanes(jnp.where(valid[None], s, NEG))
    p = p * jnp.where(qpos >= CMP_BLK - 1, 1.0, 0.0)[None]
    o_cmp = _dot(p.reshape(rows, nblk).astype(BF16), vc_ref[0, 0])

    qpos_l = qi * tq + lax.broadcasted_iota(jnp.int32, (1, tq), 1)
    blk_end_s = lax.broadcasted_iota(jnp.int32, (nblk, 1), 0) * CMP_BLK + (CMP_BLK - 1)
    valid_t = blk_end_s <= qpos_l
    s_t = _dot_nt(kc, q4)
    imp_t = jnp.zeros((nblk, tq), F32)
    for r in range(GQA_R):
        sr = jnp.where(valid_t, s_t[:, r * tq:(r + 1) * tq], NEG)
        er = jnp.exp(sr - jnp.max(sr, axis=0, keepdims=True))
        imp_t = imp_t + er / jnp.sum(er, axis=0, keepdims=True)
    imp_t = imp_t * jnp.where(qpos_l >= CMP_BLK - 1, 1.0, 0.0)
    sel_t = _select_blocks_t(imp_t, qpos_l // CMP_BLK, nsel)
    pad = e_ref.shape[0] - nblk
    if pad:
        sel_t = jnp.concatenate([sel_t, jnp.zeros((pad, tq), F32)], axis=0)
    sel = sel_t.T.astype(BF16)

    def kv_step(j, carry):
        m_i, l_i, acc = carry
        off = pl.multiple_of(j * kt, kt)
        kpos = off + lax.broadcasted_iota(jnp.int32, (1, kt), 1)
        hit = _dot(sel, e_ref[:, pl.ds(off, kt)])
        bias = jnp.where(jnp.logical_and(hit > 0.5, kpos <= qpos), 0.0, NEG)
        sj = _dot_nt(q4, ks_ref[0, 0, pl.ds(off, kt), :]).reshape(GQA_R, tq, kt) + bias[None]
        m_new = jnp.maximum(m_i, jnp.max(sj, axis=-1, keepdims=True))
        alpha = jnp.exp(m_i - m_new)
        pj = jnp.exp(sj - m_new)
        l_new = alpha * l_i + jnp.sum(pj, axis=-1, keepdims=True)
        pv = _dot(pj.reshape(rows, kt).astype(BF16), vs_ref[0, 0, pl.ds(off, kt), :])
        acc = acc * alpha.reshape(rows, 1) + pv
        return m_new, l_new, acc

    n_tiles = (qi * tq + tq + kt - 1) // kt
    init = (jnp.full((GQA_R, tq, 1), NEG, F32), jnp.zeros((GQA_R, tq, 1), F32), jnp.zeros((rows, HEAD_DIM), F32))
    _, l_f, acc = lax.fori_loop(0, n_tiles, kv_step, init)
    o_sel = acc / l_f.reshape(rows, 1)

    span = WINDOW + tq
    start = pl.multiple_of(jnp.maximum(qi * tq - WINDOW, 0), tq)
    kpos = start + lax.broadcasted_iota(jnp.int32, (1, span), 1)
    band = jnp.logical_and(kpos <= qpos, kpos > qpos - WINDOW)
    sw = _dot_nt(q4, kw_ref[0, 0, pl.ds(start, span), :]).reshape(GQA_R, tq, span)
    pw = _softmax_lanes(jnp.where(band[None], sw, NEG))
    o_win = _dot(pw.reshape(rows, span).astype(BF16), vw_ref[0, 0, pl.ds(start, span), :])

    for r in range(GQA_R):
        sl = slice(r * tq, (r + 1) * tq)
        cs = slice(r * HEAD_DIM, (r + 1) * HEAD_DIM)
        o_ref[0, :, cs] = o_cmp[sl].astype(BF16)
        o_ref[1, :, cs] = o_sel[sl].astype(BF16)
        o_ref[2, :, cs] = o_win[sl].astype(BF16)


def _attn_prompt(q_hm, kcmp, vcmp, kv_hm, expand, nbatch, seq, tq, kt):
    nq = seq // tq
    nblk = seq // CMP_BLK
    kv = lambda seg: pl.BlockSpec((1, 1, seq, HEAD_DIM), lambda b, g, i: (seg, g, b, 0))
    cmp = pl.BlockSpec((1, 1, nblk, HEAD_DIM), lambda b, g, i: (b, g, 0, 0))
    return pl.pallas_call(
        functools.partial(_attn_prompt_kernel, kt=kt, nsel=min(N_SEL, nblk)),
        grid=(nbatch, N_KV, nq),
        in_specs=[pl.BlockSpec((GQA_R, tq, HEAD_DIM), lambda b, g, i: (g, b * nq + i, 0)),
                  cmp, cmp, kv(2), kv(3), kv(4), kv(5),
                  pl.BlockSpec(expand.shape, lambda b, g, i: (0, 0))],
        out_specs=pl.BlockSpec((3, tq, GQA_R * HEAD_DIM), lambda b, g, i: (0, b * nq + i, g)),
        out_shape=jax.ShapeDtypeStruct((3, nbatch * seq, D_ATT), BF16),
        compiler_params=_cparams(("parallel", "parallel", "arbitrary")),
        name="attn_prompt",
    )(q_hm, kcmp, vcmp, kv_hm, kv_hm, kv_hm, kv_hm, expand)


def _attn_sample_kernel(pt_ref, ks_hbm, vs_hbm, q_ref, kc_ref, vc_ref, ksn_ref, vsn_ref, kwc_ref, vwc_ref,
                        kwn_ref, vwn_ref, e_ref, o_ref, okw_ref, ovw_ref, kbuf, vbuf, newbuf, ksem, vsem,
                        *, n_pages, n_tok, nsel):
    b = pl.program_id(0)
    slot = _gather_step(pt_ref, (ks_hbm, vs_hbm), (kbuf, vbuf), (ksem, vsem), n_pages)
    past = n_pages * PAGE_SIZE
    nblk_c = past // CMP_BLK
    nblk = kc_ref.shape[2]
    rows = GQA_R * n_tok
    lw = kwc_ref.shape[2]

    @pl.when(b == 0)
    def _():
        newbuf[...] = jnp.zeros(newbuf.shape, F32)

    for i, ref in enumerate((ksn_ref, vsn_ref, kwn_ref, vwn_ref)):
        newbuf[i, 0:n_tok] = ref[:, 0]

    okw_ref[0, 0, 0:lw - n_tok] = kwc_ref[0, 0, n_tok:lw]
    okw_ref[0, 0, lw - n_tok:lw] = kwn_ref[:, 0]
    ovw_ref[0, 0, 0:lw - n_tok] = vwc_ref[0, 0, n_tok:lw]
    ovw_ref[0, 0, lw - n_tok:lw] = vwn_ref[:, 0]

    t_row = lax.broadcasted_iota(jnp.int32, (rows, 1), 0) % n_tok
    qpos = past + t_row
    nidx = lax.broadcasted_iota(jnp.int32, (rows, nblk), 1)
    valid = nidx * CMP_BLK + (CMP_BLK - 1) <= qpos
    cur = qpos // CMP_BLK
    forced = jnp.logical_or(jnp.logical_or(nidx == cur, nidx == cur - 1), nidx == 0)
    new_lane = lax.broadcasted_iota(jnp.int32, (1, PAGE_SIZE), 1)
    new_ok = jnp.logical_and(new_lane < n_tok, past + new_lane <= qpos)
    wpos = (past - lw) + lax.broadcasted_iota(jnp.int32, (1, lw), 1)
    win_ok = jnp.logical_and(wpos <= qpos, wpos > qpos - WINDOW)
    win_new_ok = jnp.logical_and(new_ok, past + new_lane > qpos - WINDOW)

    for g in range(N_KV):
        q = q_ref[g, 0].astype(BF16)
        kc = kc_ref[0, g].astype(BF16)
        p = _softmax_lanes(jnp.where(valid, _dot_nt(q, kc), NEG))
        p = p * jnp.where(qpos >= CMP_BLK - 1, 1.0, 0.0)
        o_cmp = _dot(p.astype(BF16), vc_ref[0, g].astype(BF16))
        u = p + pltpu.roll(p, (GQA_R // 2) * n_tok, 0)
        imp = u + pltpu.roll(u, n_tok, 0)
        imp = jnp.where(forced, 1e4, imp)
        imp = jnp.where(nidx > cur, -1.0, imp)
        rank = jnp.zeros((rows, nblk), F32)
        for i in range(nblk):
            col = imp[:, i:i + 1]
            rank = rank + jnp.where(nidx > i, jnp.where(col >= imp, 1.0, 0.0), jnp.where(col > imp, 1.0, 0.0))
        sel = jnp.where(rank < float(nsel), 1.0, 0.0)
        hit = _dot(sel.astype(BF16), e_ref[...])
        s_c = jnp.where(hit > 0.5, _dot_nt(q, kbuf[slot, :, g, :].astype(BF16)), NEG)
        new_sel = jnp.logical_and(new_ok, sel[:, nblk_c:nblk_c + 1] > 0.5)
        s_n = jnp.where(new_sel, _dot_nt(q, newbuf[0, :, g, :].astype(BF16)), NEG)
        ps = _softmax_lanes(jnp.concatenate([s_c, s_n], axis=1)).astype(BF16)
        o_sel = _dot(ps[:, :past], vbuf[slot, :, g, :].astype(BF16)) + _dot(ps[:, past:], newbuf[1, :, g, :].astype(BF16))
        s_w = jnp.where(win_ok, _dot_nt(q, kwc_ref[0, 0, :, g, :].astype(BF16)), NEG)
        s_wn = jnp.where(win_new_ok, _dot_nt(q, newbuf[2, :, g, :].astype(BF16)), NEG)
        pw = _softmax_lanes(jnp.concatenate([s_w, s_wn], axis=1)).astype(BF16)
        o_win = _dot(pw[:, :lw], vwc_ref[0, 0, :, g, :].astype(BF16)) + _dot(pw[:, lw:], newbuf[3, :, g, :].astype(BF16))
        for j, o in enumerate((o_cmp, o_sel, o_win)):
            for r in range(GQA_R):
                h = g * GQA_R + r
                for t in range(n_tok):
                    o_ref[j, 0, t:t + 1, h * HEAD_DIM:(h + 1) * HEAD_DIM] = o[r * n_tok + t:r * n_tok + t + 1, :]


def _attn_sample(page_table, cache_ks, cache_vs, q_s, kcmp, vcmp, ks_new, vs_new, cache_kw, cache_vw, kw_new, vw_new,
                 expand):
    nb, n_pages = page_table.shape
    past = n_pages * PAGE_SIZE
    n_tok = ks_new.shape[0]
    rows = GQA_R * n_tok
    nblk = kcmp.shape[2]
    lw = cache_kw.shape[2]
    new = pl.BlockSpec((n_tok, 1, N_KV, HEAD_DIM), lambda b, pt: (0, b, 0, 0))
    cmp = pl.BlockSpec((1, N_KV, nblk, HEAD_DIM), lambda b, pt: (b, 0, 0, 0))
    win = pl.BlockSpec((1, 1, lw, N_KV, HEAD_DIM), lambda b, pt: (0, b, 0, 0, 0))
    any_ = pl.BlockSpec(memory_space=pl.ANY)
    return pl.pallas_call(
        functools.partial(_attn_sample_kernel, n_pages=n_pages, n_tok=n_tok, nsel=min(N_SEL, past // CMP_BLK + 1)),
        grid_spec=pltpu.PrefetchScalarGridSpec(
            num_scalar_prefetch=1,
            grid=(nb,),
            in_specs=[any_, any_,
                      pl.BlockSpec((N_KV, 1, rows, HEAD_DIM), lambda b, pt: (0, b, 0, 0)),
                      cmp, cmp, new, new, win, win, new, new,
                      pl.BlockSpec(expand.shape, lambda b, pt: (0, 0))],
            out_specs=[pl.BlockSpec((3, 1, n_tok, D_ATT), lambda b, pt: (0, b, 0, 0)), win, win],
            scratch_shapes=[pltpu.VMEM((2, past, N_KV, HEAD_DIM), F32), pltpu.VMEM((2, past, N_KV, HEAD_DIM), F32),
                            pltpu.VMEM((4, PAGE_SIZE, N_KV, HEAD_DIM), F32),
                            pltpu.SemaphoreType.DMA((2,)), pltpu.SemaphoreType.DMA((2,))]),
        out_shape=[jax.ShapeDtypeStruct((3, nb, n_tok, D_ATT), F32),
                   jax.ShapeDtypeStruct(cache_kw.shape, F32), jax.ShapeDtypeStruct(cache_vw.shape, F32)],
        compiler_params=_cparams(("arbitrary",)),
        name="attn_sample",
    )(page_table, cache_ks, cache_vs, q_s, kcmp, vcmp, ks_new, vs_new, cache_kw, cache_vw, kw_new, vw_new, expand)


def _expand_matrix(rows, nkeys):
    return (jnp.arange(rows, dtype=jnp.int32)[:, None] == (jnp.arange(nkeys, dtype=jnp.int32) // CMP_BLK)[None, :]).astype(BF16)


def _pick(m, pref):
    while m % pref:
        pref //= 2
    return pref


def kernel(x_prompt, x_sample, cache_k_cmp, cache_v_cmp, cache_k_sel, cache_v_sel, cache_k_win, cache_v_win, state_h, state_conv, page_table, norm_mix_pre, w_in, conv_w, conv_b, w_ga, b_ga, w_gx, b_gx, lam, pe_k, pe_v, w_phi_k, w_phi_v, norm_rnn_out, norm_att_out, w_out, norm_mix_post, norm_mlp_pre, w_up, w_down, norm_mlp_post):
    depth = w_in.shape[0]
    assert depth == 1, "single layer only"
    bp, seq, d_model = x_prompt.shape
    bs, n_tok, _ = x_sample.shape
    d_rnn = conv_w.shape[-1]
    n_pages = page_table.shape[1]
    past = n_pages * PAGE_SIZE
    lw_in = cache_k_win.shape[2]
    assert lw_in == WINDOW and past % CMP_BLK == 0 and n_tok <= CMP_BLK and d_rnn == RNN_BLOCKS * RNN_BW
    l = 0

    w = w_in[l]
    c0 = 2 * d_rnn
    w_xy = w[:, :c0].astype(BF16)
    w_q = w[:, c0:c0 + D_ATT].astype(BF16)
    w_kv = w[:, c0 + D_ATT:c0 + D_ATT + 6 * D_KV].astype(BF16)
    w_gl = jnp.pad(w[:, c0 + D_ATT + 6 * D_KV:], ((0, 0), (0, LANES - 3 * N_HEADS))).astype(BF16)
    w_o = w_out[l].astype(BF16)
    w_u = w_up[l].astype(BF16)
    w_d = w_down[l].astype(BF16)
    wg = jnp.concatenate([w_ga[l], w_gx[l]], axis=-1).astype(BF16)
    wk_phi = w_phi_k[l].astype(BF16)
    wv_phi = w_phi_v[l].astype(BF16)
    vec = lambda a: a[l].reshape(1, -1)
    rnn_p = (conv_w[l], vec(conv_b), wg, vec(b_ga), vec(b_gx), vec(lam))

    def mixer_tail(x2d, o_rnn, o3, gates, tm):
        cat = _cat_norm(o_rnn, o3, gates, vec(norm_rnn_out), vec(norm_att_out), _pick(x2d.shape[0], 256))
        mix = _matmul(cat, w_o, tm, 512, name="out_proj")
        h, hn = _post_norm(mix, x2d, vec(norm_mix_post), vec(norm_mlp_pre), _pick(x2d.shape[0], 256), True)
        ff = _mlp(hn, w_u, w_d, _pick(x2d.shape[0], 512), 512)
        (y,) = _post_norm(ff, h, vec(norm_mlp_post), vec(norm_mlp_post), _pick(x2d.shape[0], 256), False)
        return y

    mp = bp * seq
    xp2 = x_prompt.reshape(mp, d_model)
    tm_p = _pick(seq, 1024)
    cos_p, sin_p = _rope_tables(jnp.arange(seq, dtype=jnp.int32))
    xn = _norm_cast(xp2, vec(norm_mix_pre), _pick(mp, 512))
    xy = _matmul(xn, w_xy, tm_p, 512, name="proj_xy")
    q_hm = _proj_q_prompt(xn, w_q, cos_p, sin_p, tm_p)
    *kv_nat, kv_hm = _proj_kv(xn, w_kv, cos_p, sin_p, _pick(seq, 512), True)
    gates = _matmul(xn, w_gl, tm_p, LANES, body=_mm_gate_kernel, name="proj_gate")
    o_rnn, h_p, cv_p = _rglru_prompt(xy, *rnn_p, bp, seq, _pick(seq, 128))
    nblk_p = seq // CMP_BLK
    cos_e, sin_e = _rope_tables(jnp.arange(nblk_p, dtype=jnp.int32) * CMP_BLK + CMP_BLK - 1)
    cmp_rows = _pick(seq, 1024)
    kcmp, vcmp = _cmp_prompt(kv_nat[0], kv_nat[1], pe_k[l], pe_v[l], wk_phi, wv_phi, cos_e, sin_e, bp, seq, cmp_rows)
    e_rows = max(LANES, nblk_p)
    o3 = _attn_prompt(q_hm, kcmp, vcmp, kv_hm, _expand_matrix(e_rows, seq), bp, seq, 128, _pick(seq, 512))
    y_p = mixer_tail(xp2, o_rnn, o3, gates, tm_p).reshape(bp, seq, d_model)
    lw_p = min(WINDOW, seq)
    nat5 = lambda a: a.reshape(1, bp, seq, N_KV, HEAD_DIM)
    p_out = (nat5(kv_nat[0]), nat5(kv_nat[1]), nat5(kv_nat[2]), nat5(kv_nat[3]),
             nat5(kv_nat[4])[:, :, seq - lw_p:], nat5(kv_nat[5])[:, :, seq - lw_p:],
             h_p.reshape(1, bp, d_rnn), cv_p.reshape(1, bp, CONV_W - 1, d_rnn))

    ms = bs * n_tok
    xs_tb = x_sample.swapaxes(0, 1).reshape(ms, d_model)
    pos_s = past + jnp.arange(n_tok, dtype=jnp.int32)
    cos_s, sin_s = _rope_tables(jnp.repeat(pos_s, bs))
    xn_s = _norm_cast(xs_tb, vec(norm_mix_pre), ms)
    xy_s = _matmul(xn_s, w_xy, ms, 512, name="proj_xy")
    q_s = _proj_q_sample(xn_s, w_q, cos_s, sin_s, n_tok)
    kv_s = _proj_kv(xn_s, w_kv, cos_s, sin_s, ms, False)
    gates_s = _matmul(xn_s, w_gl, ms, LANES, body=_mm_gate_kernel, name="proj_gate")
    o_rnn_s, h_s, cv_s = _rglru_sample(xy_s, state_conv[l].swapaxes(0, 1), state_h[l], *rnn_p, n_tok, past)
    kv_s4 = [a.reshape(n_tok, bs, N_KV, HEAD_DIM) for a in kv_s]
    nblk_c = past // CMP_BLK
    npad = 8
    cos_es, sin_es = _rope_tables(jnp.arange(nblk_c + npad, dtype=jnp.int32) * CMP_BLK + CMP_BLK - 1)
    kcmp_s, vcmp_s = _cmp_sample(page_table, cache_k_cmp, cache_v_cmp, kv_s4[0], kv_s4[1], pe_k[l], pe_v[l],
                                 wk_phi, wv_phi, cos_es, sin_es, npad)
    o3_s, kw_out, vw_out = _attn_sample(page_table, cache_k_sel, cache_v_sel, q_s, kcmp_s, vcmp_s, kv_s4[2], kv_s4[3],
                                        cache_k_win, cache_v_win, kv_s4[4], kv_s4[5],
                                        _expand_matrix(nblk_c + npad, past))
    gates_bt = gates_s.reshape(n_tok, bs, LANES).swapaxes(0, 1).reshape(ms, LANES)
    y_s = mixer_tail(x_sample.reshape(ms, d_model), o_rnn_s.reshape(ms, d_rnn), o3_s.reshape(3, ms, D_ATT), gates_bt,
                     ms).reshape(bs, n_tok, d_model)
    new5 = lambda a: a.swapaxes(0, 1).reshape(1, bs, n_tok, N_KV, HEAD_DIM)
    s_out = (new5(kv_s4[0]), new5(kv_s4[1]), new5(kv_s4[2]), new5(kv_s4[3]), kw_out, vw_out,
             h_s.reshape(1, bs, d_rnn), cv_s.swapaxes(0, 1).reshape(1, bs, CONV_W - 1, d_rnn))
    return (y_p, y_s) + p_out + s_out
```

```python
import functools

import jax
import jax.numpy as jnp
import numpy as np
from jax import lax
from jax.experimental import pallas as pl
from jax.experimental.pallas import tpu as pltpu

F32 = jnp.float32
BF16 = jnp.bfloat16

HEAD_DIM = 128
N_KV = 4
GQA_R = 4
N_HEADS = N_KV * GQA_R
D_ATT = N_HEADS * HEAD_DIM
D_KV = N_KV * HEAD_DIM
CMP_BLK = 64
N_SEL = 16
WINDOW = 512
CONV_W = 4
C_GATE = 8.0
RNN_BLOCKS = 16
RNN_BW = 128
ROPE_THETA = 10000.0
EPS = 1e-6
NEG = -1e30
PAGE_SIZE = 128
SCALE = HEAD_DIM ** -0.5
LANES = 128
VMEM_LIMIT = 56 * 1024 * 1024


def _cparams(sem):
    return pltpu.CompilerParams(dimension_semantics=sem, vmem_limit_bytes=VMEM_LIMIT)


def _rope_tables(pos):
    half = HEAD_DIM // 2
    inv = ROPE_THETA ** (-jnp.arange(half, dtype=F32) * (2.0 / HEAD_DIM))
    ang = pos.astype(F32)[:, None] * inv[None, :]
    c, s = jnp.cos(ang), jnp.sin(ang)
    return jnp.concatenate([c, c], axis=-1), jnp.concatenate([-s, s], axis=-1)


def _rope(x, cos2, sin2):
    return x * cos2 + pltpu.roll(x, HEAD_DIM // 2, 1) * sin2


def _rms(x):
    return x * lax.rsqrt(jnp.mean(x * x, axis=-1, keepdims=True) + EPS)


def _gelu(x):
    return 0.5 * x * (1.0 + jnp.tanh(0.7978845608028654 * (x + 0.044715 * (x * x * x))))


def _softplus(x):
    return jnp.maximum(x, 0.0) + jnp.log1p(jnp.exp(-jnp.abs(x)))


def _sigmoid(x):
    return 1.0 / (1.0 + jnp.exp(-x))


def _norm_kernel(x_ref, g_ref, o_ref):
    o_ref[...] = (_rms(x_ref[...]) * g_ref[...]).astype(o_ref.dtype)


def _norm_cast(x, g, tm):
    m, d = x.shape
    return pl.pallas_call(
        _norm_kernel,
        grid=(m // tm,),
        in_specs=[pl.BlockSpec((tm, d), lambda i: (i, 0)), pl.BlockSpec((1, d), lambda i: (0, 0))],
        out_specs=pl.BlockSpec((tm, d), lambda i: (i, 0)),
        out_shape=jax.ShapeDtypeStruct((m, d), BF16),
        compiler_params=_cparams(("parallel",)),
        name="norm_cast",
    )(x, g)


def _post_kernel(acc_ref, res_ref, g_ref, g2_ref, y_ref, *hn_ref):
    y = res_ref[...] + _rms(acc_ref[...]) * g_ref[...]
    y_ref[...] = y
    if hn_ref:
        hn_ref[0][...] = (_rms(y) * g2_ref[...]).astype(BF16)


def _post_norm(acc, res, g, g2, tm, with_hn):
    m, d = acc.shape
    row = pl.BlockSpec((tm, d), lambda i: (i, 0))
    vec = pl.BlockSpec((1, d), lambda i: (0, 0))
    out_shape = [jax.ShapeDtypeStruct((m, d), F32)]
    out_specs = [row]
    if with_hn:
        out_shape.append(jax.ShapeDtypeStruct((m, d), BF16))
        out_specs.append(row)
    return pl.pallas_call(
        _post_kernel,
        grid=(m // tm,),
        in_specs=[row, row, vec, vec],
        out_specs=out_specs,
        out_shape=out_shape,
        compiler_params=_cparams(("parallel",)),
        name="post_norm",
    )(acc, res, g, g2)


def _cat_norm_kernel(orn_ref, o3_ref, gate_ref, grn_ref, gat_ref, cat_ref, tmp):
    d = orn_ref.shape[1]
    cat_ref[:, :d] = (_rms(orn_ref[...].astype(F32)) * grn_ref[...]).astype(BF16)
    g = gate_ref[...]
    ss = jnp.zeros((g.shape[0], 1), F32)
    for h in range(N_HEADS):
        sl = slice(h * HEAD_DIM, (h + 1) * HEAD_DIM)
        oh = (g[:, 3 * h:3 * h + 1] * o3_ref[0, :, sl].astype(F32)
              + g[:, 3 * h + 1:3 * h + 2] * o3_ref[1, :, sl].astype(F32)
              + g[:, 3 * h + 2:3 * h + 3] * o3_ref[2, :, sl].astype(F32))
        tmp[:, sl] = oh
        ss = ss + jnp.sum(oh * oh, axis=-1, keepdims=True)
    inv = lax.rsqrt(ss * (1.0 / D_ATT) + EPS)
    cat_ref[:, d:] = (tmp[...] * inv * gat_ref[...]).astype(BF16)


def _cat_norm(o_rnn, o3, gates, g_rnn, g_att, tm):
    m, d = o_rnn.shape
    return pl.pallas_call(
        _cat_norm_kernel,
        grid=(m // tm,),
        in_specs=[pl.BlockSpec((tm, d), lambda i: (i, 0)),
                  pl.BlockSpec((3, tm, D_ATT), lambda i: (0, i, 0)),
                  pl.BlockSpec((tm, LANES), lambda i: (i, 0)),
                  pl.BlockSpec((1, d), lambda i: (0, 0)),
                  pl.BlockSpec((1, D_ATT), lambda i: (0, 0))],
        out_specs=pl.BlockSpec((tm, d + D_ATT), lambda i: (i, 0)),
        out_shape=jax.ShapeDtypeStruct((m, d + D_ATT), BF16),
        scratch_shapes=[pltpu.VMEM((tm, D_ATT), F32)],
        compiler_params=_cparams(("parallel",)),
        name="cat_norm",
    )(o_rnn, o3, gates, g_rnn, g_att)


def _dot(a, b):
    return jnp.dot(a, b, preferred_element_type=F32)


def _dot_nt(a, b):
    return lax.dot_general(a, b, (((1,), (1,)), ((), ())), preferred_element_type=F32)


def _mm_plain_kernel(a_ref, b_ref, o_ref):
    o_ref[...] = _dot(a_ref[...], b_ref[...])


def _mm_gate_kernel(a_ref, b_ref, o_ref):
    o_ref[...] = _sigmoid(_dot(a_ref[...], b_ref[...]))


def _matmul(a, b, tm, tn, body=_mm_plain_kernel, name="matmul", n=None, col0=0):
    m, k = a.shape
    n = b.shape[1] if n is None else n
    return pl.pallas_call(
        body,
        grid=(m // tm, n // tn),
        in_specs=[pl.BlockSpec((tm, k), lambda i, j: (i, 0)), pl.BlockSpec((k, tn), lambda i, j: (0, col0 + j))],
        out_specs=pl.BlockSpec((tm, tn), lambda i, j: (i, j)),
        out_shape=jax.ShapeDtypeStruct((m, n), F32),
        compiler_params=_cparams(("parallel", "arbitrary")),
        name=name,
    )(a, b)


def _mm_q_prompt_kernel(a_ref, b_ref, cos_ref, sin_ref, o_ref):
    acc = _dot(a_ref[...], b_ref[...])
    cos2, sin2 = cos_ref[...], sin_ref[...]
    for r in range(GQA_R):
        x = acc[:, r * HEAD_DIM:(r + 1) * HEAD_DIM]
        o_ref[r] = (_rope(x, cos2, sin2) * SCALE).astype(BF16)


def _proj_q_prompt(a, w, col0, cos2, sin2, tm):
    m, k = a.shape
    nt = cos2.shape[0] // tm
    return pl.pallas_call(
        _mm_q_prompt_kernel,
        grid=(m // tm, N_KV),
        in_specs=[pl.BlockSpec((tm, k), lambda i, j: (i, 0)),
                  pl.BlockSpec((k, GQA_R * HEAD_DIM), lambda i, j: (0, col0 + j)),
                  pl.BlockSpec((tm, HEAD_DIM), lambda i, j: (i % nt, 0)),
                  pl.BlockSpec((tm, HEAD_DIM), lambda i, j: (i % nt, 0))],
        out_specs=pl.BlockSpec((GQA_R, tm, HEAD_DIM), lambda i, j: (j, i, 0)),
        out_shape=jax.ShapeDtypeStruct((N_HEADS, m, HEAD_DIM), BF16),
        compiler_params=_cparams(("parallel", "arbitrary")),
        name="proj_q_prompt",
    )(a, w, cos2, sin2)


def _mm_q_sample_kernel(a_ref, b_ref, cos_ref, sin_ref, o_ref, tmp, *, n_tok):
    acc = _dot(a_ref[...], b_ref[...])
    cos2, sin2 = cos_ref[...], sin_ref[...]
    nb = o_ref.shape[1]
    for r in range(GQA_R):
        x = acc[:, r * HEAD_DIM:(r + 1) * HEAD_DIM]
        tmp[...] = _rope(x, cos2, sin2) * SCALE
        for t in range(n_tok):
            o_ref[0, :, n_tok * r + t, :] = tmp[t * nb:(t + 1) * nb, :]


def _proj_q_sample(a, w, col0, cos2, sin2, n_tok):
    m, k = a.shape
    nb = m // n_tok
    return pl.pallas_call(
        functools.partial(_mm_q_sample_kernel, n_tok=n_tok),
        grid=(N_KV,),
        in_specs=[pl.BlockSpec((m, k), lambda j: (0, 0)),
                  pl.BlockSpec((k, GQA_R * HEAD_DIM), lambda j: (0, col0 + j)),
                  pl.BlockSpec((m, HEAD_DIM), lambda j: (0, 0)),
                  pl.BlockSpec((m, HEAD_DIM), lambda j: (0, 0))],
        out_specs=pl.BlockSpec((1, nb, GQA_R * n_tok, HEAD_DIM), lambda j: (j, 0, 0, 0)),
        out_shape=jax.ShapeDtypeStruct((N_KV, nb, GQA_R * n_tok, HEAD_DIM), F32),
        scratch_shapes=[pltpu.VMEM((m, HEAD_DIM), F32)],
        compiler_params=_cparams(("arbitrary",)),
        name="proj_q_sample",
    )(a, w, cos2, sin2)


def _mm_kv_kernel(a_ref, b_ref, cos_ref, sin_ref, *outs, with_hm):
    j = pl.program_id(1)
    tm = a_ref.shape[0]
    acc = _dot(a_ref[...], b_ref[...])
    is_rope = jnp.logical_or(j == 2, j == 4)
    cos2 = jnp.where(is_rope, cos_ref[...], 1.0)
    sin2 = jnp.where(is_rope, sin_ref[...], 0.0)
    ys = [_rope(acc[:, g * HEAD_DIM:(g + 1) * HEAD_DIM], cos2, sin2) for g in range(N_KV)]
    for s in range(6):
        @pl.when(j == s)
        def _(s=s):
            for g in range(N_KV):
                outs[s][pl.ds(g, tm, stride=N_KV), :] = ys[g]
    if with_hm:
        for g in range(N_KV):
            outs[6][0, g] = ys[g].astype(BF16)


def _proj_kv(a, w, col0, cos2, sin2, tm, with_hm):
    m, k = a.shape
    nt = cos2.shape[0] // tm
    nat = pl.BlockSpec((tm * N_KV, HEAD_DIM), lambda i, j: (i, 0))
    out_specs = [nat] * 6
    out_shape = [jax.ShapeDtypeStruct((m * N_KV, HEAD_DIM), F32)] * 6
    if with_hm:
        out_specs = out_specs + [pl.BlockSpec((1, N_KV, tm, HEAD_DIM), lambda i, j: (j, 0, i, 0))]
        out_shape = out_shape + [jax.ShapeDtypeStruct((6, N_KV, m, HEAD_DIM), BF16)]
    return pl.pallas_call(
        functools.partial(_mm_kv_kernel, with_hm=with_hm),
        grid=(m // tm, 6),
        in_specs=[pl.BlockSpec((tm, k), lambda i, j: (i, 0)),
                  pl.BlockSpec((k, D_KV), lambda i, j: (0, col0 + j)),
                  pl.BlockSpec((tm, HEAD_DIM), lambda i, j: (i % nt, 0)),
                  pl.BlockSpec((tm, HEAD_DIM), lambda i, j: (i % nt, 0))],
        out_specs=out_specs,
        out_shape=out_shape,
        compiler_params=_cparams(("parallel", "arbitrary")),
        name="proj_kv",
    )(a, w, cos2, sin2)


def _mlp_kernel(h_ref, wu_ref, wd_ref, o_ref):
    j = pl.program_id(1)
    u = _dot(h_ref[...], wu_ref[...])
    f = jnp.square(jnp.maximum(u, 0.0)).astype(BF16)
    d = _dot(f, wd_ref[...])

    @pl.when(j == 0)
    def _():
        o_ref[...] = d

    @pl.when(j > 0)
    def _():
        o_ref[...] += d


def _mlp(hn, w_up, w_down, tm, tf):
    m, d = hn.shape
    f = w_up.shape[1]
    return pl.pallas_call(
        _mlp_kernel,
        grid=(m // tm, f // tf),
        in_specs=[pl.BlockSpec((tm, d), lambda i, j: (i, 0)),
                  pl.BlockSpec((d, tf), lambda i, j: (0, j)),
                  pl.BlockSpec((tf, d), lambda i, j: (j, 0))],
        out_specs=pl.BlockSpec((tm, d), lambda i, j: (i, 0)),
        out_shape=jax.ShapeDtypeStruct((m, d), F32),
        compiler_params=_cparams(("parallel", "arbitrary")),
        name="mlp",
    )(hn, w_up, w_down)


def _rglru_gates(xc, gg, bga, bgx, lam, first_row):
    ra = _sigmoid(gg[:, :RNN_BW] + bga)
    rx = _sigmoid(gg[:, RNN_BW:] + bgx)
    log_a = (-C_GATE) * ra * _softplus(-lam)
    a = jnp.exp(log_a)
    th = jnp.tanh(log_a)
    mult = jnp.sqrt((-2.0) * th / (1.0 - th))
    if first_row is not None:
        mult = jnp.where(first_row, 1.0, mult)
    return a, mult * rx * xc


def _rglru_prompt_kernel(xr_ref, yr_ref, cw_ref, cb_ref, wg_ref, bga_ref, bgx_ref, lam_ref,
                         o_ref, h_ref, conv_ref, xbuf, hcar):
    ti = pl.program_id(1)
    tt = xr_ref.shape[0]

    @pl.when(ti == 0)
    def _():
        xbuf[0:8, :] = jnp.zeros((8, xbuf.shape[1]), F32)
        hcar[...] = jnp.zeros(hcar.shape, F32)

    xbuf[8:8 + tt, :] = xr_ref[...]
    row = lax.broadcasted_iota(jnp.int32, (tt, RNN_BW), 0)
    first_row = jnp.logical_and(row == 0, ti == 0)

    def block(n, carry):
        col = pl.ds(pl.multiple_of(n * RNN_BW, RNN_BW), RNN_BW)
        xc = cb_ref[:, col]
        for k in range(CONV_W):
            xc = xc + xbuf[8 - (CONV_W - 1) + k:8 - (CONV_W - 1) + k + tt, col] * cw_ref[k:k + 1, col]
        gg = _dot(xc.astype(BF16), wg_ref[n])
        a, b = _rglru_gates(xc, gg, bga_ref[:, col], bgx_ref[:, col], lam_ref[:, col], first_row)
        s = 1
        while s < tt:
            keep = row >= s
            a_sh = jnp.where(keep, pltpu.roll(a, s, 0), 1.0)
            b_sh = jnp.where(keep, pltpu.roll(b, s, 0), 0.0)
            b = a * b_sh + b
            a = a * a_sh
            s *= 2
        hs = a * hcar[:, col] + b
        hcar[:, col] = hs[tt - 1:tt, :]
        o_ref[:, col] = (hs * _gelu(yr_ref[:, col])).astype(o_ref.dtype)
        return carry

    lax.fori_loop(0, RNN_BLOCKS, block, 0)
    xbuf[0:8, :] = xbuf[tt:tt + 8, :]

    @pl.when(ti == pl.num_programs(1) - 1)
    def _():
        h_ref[0] = hcar[...]
        conv_ref[0] = xbuf[8 - (CONV_W - 1):8, :]


def _rglru_prompt(xy, conv_w, conv_b, wg, b_ga, b_gx, lam, nbatch, seq, tt):
    d = conv_w.shape[1]
    nt = seq // tt
    vec = pl.BlockSpec((1, d), lambda b, t: (0, 0))
    return pl.pallas_call(
        _rglru_prompt_kernel,
        grid=(nbatch, nt),
        in_specs=[pl.BlockSpec((tt, d), lambda b, t: (b * nt + t, 0)),
                  pl.BlockSpec((tt, d), lambda b, t: (b * nt + t, 1)),
                  pl.BlockSpec((CONV_W, d), lambda b, t: (0, 0)),
                  vec,
                  pl.BlockSpec((RNN_BLOCKS, RNN_BW, 2 * RNN_BW), lambda b, t: (0, 0, 0)),
                  vec, vec, vec],
        out_specs=[pl.BlockSpec((tt, d), lambda b, t: (b * nt + t, 0)),
                   pl.BlockSpec((1, 1, d), lambda b, t: (b, 0, 0)),
                   pl.BlockSpec((1, CONV_W - 1, d), lambda b, t: (b, 0, 0))],
        out_shape=[jax.ShapeDtypeStruct((nbatch * seq, d), BF16),
                   jax.ShapeDtypeStruct((nbatch, 1, d), F32),
                   jax.ShapeDtypeStruct((nbatch, CONV_W - 1, d), F32)],
        scratch_shapes=[pltpu.VMEM((tt + 8, d), F32), pltpu.VMEM((1, d), F32)],
        compiler_params=_cparams(("arbitrary", "arbitrary")),
        name="rglru_prompt",
    )(xy, xy, conv_w, conv_b, wg, b_ga, b_gx, lam)


def _rglru_sample_kernel(xr_ref, yr_ref, cst_ref, h0_ref, cw_ref, cb_ref, wg_ref, bga_ref, bgx_ref, lam_ref,
                         o_ref, h_ref, conv_ref, *, n_tok, pos0):
    nb = h0_ref.shape[0]
    xs = [cst_ref[k] for k in range(CONV_W - 1)] + [xr_ref[t * nb:(t + 1) * nb, :] for t in range(n_tok)]
    h = h0_ref[...]
    for t in range(n_tok):
        xc = cb_ref[...]
        for k in range(CONV_W):
            xc = xc + xs[t + k] * cw_ref[k:k + 1, :]
        gg = _dot(xc.astype(BF16), wg_ref[0])
        a, b = _rglru_gates(xc, gg, bga_ref[...], bgx_ref[...], lam_ref[...], True if pos0 + t == 0 else None)
        h = a * h + b
        o_ref[:, t, :] = h * _gelu(yr_ref[t * nb:(t + 1) * nb, :])
    h_ref[...] = h
    for k in range(CONV_W - 1):
        conv_ref[k] = xs[n_tok + k]


def _rglru_sample(xy, cst, h0, conv_w, conv_b, wg, b_ga, b_gx, lam, n_tok, pos0):
    nb, d = h0.shape
    m = nb * n_tok
    vec = pl.BlockSpec((1, RNN_BW), lambda n: (0, n))
    st = pl.BlockSpec((CONV_W - 1, nb, RNN_BW), lambda n: (0, 0, n))
    return pl.pallas_call(
        functools.partial(_rglru_sample_kernel, n_tok=n_tok, pos0=pos0),
        grid=(RNN_BLOCKS,),
        in_specs=[pl.BlockSpec((m, RNN_BW), lambda n: (0, n)),
                  pl.BlockSpec((m, RNN_BW), lambda n: (0, RNN_BLOCKS + n)),
                  st,
                  pl.BlockSpec((nb, RNN_BW), lambda n: (0, n)),
                  pl.BlockSpec((CONV_W, RNN_BW), lambda n: (0, n)),
                  vec,
                  pl.BlockSpec((1, RNN_BW, 2 * RNN_BW), lambda n: (n, 0, 0)),
                  vec, vec, vec],
        out_specs=[pl.BlockSpec((nb, n_tok, RNN_BW), lambda n: (0, 0, n)),
                   pl.BlockSpec((nb, RNN_BW), lambda n: (0, n)),
                   st],
        out_shape=[jax.ShapeDtypeStruct((nb, n_tok, d), F32),
                   jax.ShapeDtypeStruct((nb, d), F32),
                   jax.ShapeDtypeStruct((CONV_W - 1, nb, d), F32)],
        compiler_params=_cparams(("parallel",)),
        name="rglru_sample",
    )(xy, xy, cst, h0, conv_w, conv_b, wg, b_ga, b_gx, lam)


def _block_summary(x, pe_ref, w_ref, nblk):
    m = jnp.sum(x.reshape(nblk, CMP_BLK, HEAD_DIM), axis=1) * (1.0 / CMP_BLK)
    m = m + jnp.mean(pe_ref[...], axis=0, keepdims=True)
    return _dot(m.astype(BF16), w_ref[...])


def _cmp_prompt_kernel(kc_ref, vc_ref, pek_ref, pev_ref, wk_ref, wv_ref, cos_ref, sin_ref, ko_ref, vo_ref):
    nblk = ko_ref.shape[2]
    rows = nblk * CMP_BLK
    for g in range(N_KV):
        kk = _block_summary(kc_ref[pl.ds(g, rows, stride=N_KV), :], pek_ref, wk_ref, nblk)
        ko_ref[0, g] = _rope(kk, cos_ref[...], sin_ref[...]).astype(BF16)
        vo_ref[0, g] = _block_summary(vc_ref[pl.ds(g, rows, stride=N_KV), :], pev_ref, wv_ref, nblk).astype(BF16)


def _cmp_prompt(kc, vc, pe_k, pe_v, w_k, w_v, cos_e, sin_e, nbatch, seq, rows):
    nblk = rows // CMP_BLK
    nt = seq // rows
    nat = pl.BlockSpec((rows * N_KV, HEAD_DIM), lambda b, t: (b * nt + t, 0))
    full = lambda shape: pl.BlockSpec(shape, lambda b, t: (0,) * len(shape))
    tab = pl.BlockSpec((nblk, HEAD_DIM), lambda b, t: (t, 0))
    out = pl.BlockSpec((1, N_KV, nblk, HEAD_DIM), lambda b, t: (b, 0, t, 0))
    shape = jax.ShapeDtypeStruct((nbatch, N_KV, seq // CMP_BLK, HEAD_DIM), BF16)
    return pl.pallas_call(
        _cmp_prompt_kernel,
        grid=(nbatch, nt),
        in_specs=[nat, nat, full((CMP_BLK, HEAD_DIM)), full((CMP_BLK, HEAD_DIM)),
                  full((HEAD_DIM, HEAD_DIM)), full((HEAD_DIM, HEAD_DIM)), tab, tab],
        out_specs=[out, out],
        out_shape=[shape, shape],
        compiler_params=_cparams(("parallel", "arbitrary")),
        name="cmp_prompt",
    )(kc, vc, pe_k, pe_v, w_k, w_v, cos_e, sin_e)


def _page_copies(pt_ref, hbm, buf, sem, bb, slot, n_pages):
    pr = PAGE_SIZE * N_KV
    return [pltpu.make_async_copy(hbm.at[pl.ds(pl.multiple_of(pt_ref[bb, p] * pr, pr), pr)],
                                  buf.at[slot, pl.ds(p * pr, pr)], sem.at[slot])
            for p in range(n_pages)]


def _gather_step(pt_ref, hbms, bufs, sems, n_pages):
    b = pl.program_id(0)
    slot = b % 2

    @pl.when(b == 0)
    def _():
        for hbm, buf, sem in zip(hbms, bufs, sems):
            for c in _page_copies(pt_ref, hbm, buf, sem, 0, 0, n_pages):
                c.start()

    @pl.when(b + 1 < pl.num_programs(0))
    def _():
        for hbm, buf, sem in zip(hbms, bufs, sems):
            for c in _page_copies(pt_ref, hbm, buf, sem, b + 1, 1 - slot, n_pages):
                c.start()

    for hbm, buf, sem in zip(hbms, bufs, sems):
        for c in _page_copies(pt_ref, hbm, buf, sem, b, slot, n_pages):
            c.wait()
    return slot


def _cmp_sample_kernel(pt_ref, kc_hbm, vc_hbm, kn_ref, vn_ref, pek_ref, pev_ref, wk_ref, wv_ref, cos_ref, sin_ref,
                       ko_ref, vo_ref, kbuf, vbuf, ksem, vsem, *, n_pages):
    slot = _gather_step(pt_ref, (kc_hbm, vc_hbm), (kbuf, vbuf), (ksem, vsem), n_pages)
    past = n_pages * PAGE_SIZE
    nblk = past // CMP_BLK
    npad = ko_ref.shape[2] - nblk
    n_tok = kn_ref.shape[0] // N_KV

    def tail(new_ref, pe_ref, w_ref, g):
        m = (jnp.sum(new_ref[pl.ds(g, n_tok, stride=N_KV), :], axis=0, keepdims=True)
             + jnp.sum(pe_ref[...], axis=0, keepdims=True)) * (1.0 / CMP_BLK)
        return _dot(jnp.broadcast_to(m, (npad, HEAD_DIM)).astype(BF16), w_ref[...])

    for g in range(N_KV):
        kk = _block_summary(kbuf[slot, pl.ds(g, past, stride=N_KV), :], pek_ref, wk_ref, nblk)
        ko_ref[0, g, 0:nblk, :] = _rope(kk, cos_ref[0:nblk, :], sin_ref[0:nblk, :])
        kt = tail(kn_ref, pek_ref, wk_ref, g)
        ko_ref[0, g, nblk:nblk + npad, :] = _rope(kt, cos_ref[nblk:nblk + npad, :], sin_ref[nblk:nblk + npad, :])
        vo_ref[0, g, 0:nblk, :] = _block_summary(vbuf[slot, pl.ds(g, past, stride=N_KV), :], pev_ref, wv_ref, nblk)
        vo_ref[0, g, nblk:nblk + npad, :] = tail(vn_ref, pev_ref, wv_ref, g)


def _cmp_sample(page_table, cache_k, cache_v, k_new, v_new, pe_k, pe_v, w_k, w_v, cos_e, sin_e, npad):
    nb, n_pages = page_table.shape
    past = n_pages * PAGE_SIZE
    nblk = past // CMP_BLK
    n_tok = k_new.shape[0] // (nb * N_KV)
    full = lambda shape: pl.BlockSpec(shape, lambda b, pt: (0,) * len(shape))
    new = pl.BlockSpec((n_tok * N_KV, HEAD_DIM), lambda b, pt: (b, 0))
    out = pl.BlockSpec((1, N_KV, nblk + npad, HEAD_DIM), lambda b, pt: (b, 0, 0, 0))
    shape = jax.ShapeDtypeStruct((nb, N_KV, nblk + npad, HEAD_DIM), F32)
    return pl.pallas_call(
        functools.partial(_cmp_sample_kernel, n_pages=n_pages),
        grid_spec=pltpu.PrefetchScalarGridSpec(
            num_scalar_prefetch=1,
            grid=(nb,),
            in_specs=[pl.BlockSpec(memory_space=pl.ANY), pl.BlockSpec(memory_space=pl.ANY), new, new,
                      full((CMP_BLK, HEAD_DIM)), full((CMP_BLK, HEAD_DIM)),
                      full((HEAD_DIM, HEAD_DIM)), full((HEAD_DIM, HEAD_DIM)),
                      full((nblk + npad, HEAD_DIM)), full((nblk + npad, HEAD_DIM))],
            out_specs=[out, out],
            scratch_shapes=[pltpu.VMEM((2, past * N_KV, HEAD_DIM), F32), pltpu.VMEM((2, past * N_KV, HEAD_DIM), F32),
                            pltpu.SemaphoreType.DMA((2,)), pltpu.SemaphoreType.DMA((2,))]),
        out_shape=[shape, shape],
        compiler_params=_cparams(("arbitrary",)),
        name="cmp_sample",
    )(page_table, cache_k, cache_v, k_new, v_new, pe_k, pe_v, w_k, w_v, cos_e, sin_e)


def _select_blocks_t(imp_t, cur, nsel):
    nblk = imp_t.shape[0]
    nidx = lax.broadcasted_iota(jnp.int32, imp_t.shape, 0)
    forced = jnp.logical_or(jnp.logical_or(nidx == cur, nidx == cur - 1), nidx == 0)
    imp_t = jnp.where(forced, 1e4, imp_t)
    imp_t = jnp.where(nidx > cur, -1.0, imp_t)
    rank = jnp.zeros(imp_t.shape, F32)
    for i in range(nblk):
        row = imp_t[i:i + 1, :]
        rank = rank + jnp.where(nidx > i, jnp.where(row >= imp_t, 1.0, 0.0), jnp.where(row > imp_t, 1.0, 0.0))
    return jnp.where(rank < float(nsel), 1.0, 0.0)


def _softmax_lanes(s):
    m = jnp.max(s, axis=-1, keepdims=True)
    e = jnp.exp(s - m)
    return e / jnp.sum(e, axis=-1, keepdims=True)


def _attn_prompt_kernel(q_ref, kc_ref, vc_ref, ks_ref, vs_ref, kw_ref, vw_ref, e_ref, o_ref, *, kt, nsel):
    qi = pl.program_id(2)
    tq = q_ref.shape[1]
    nblk = kc_ref.shape[2]
    hh = GQA_R
    nch = GQA_R // hh
    rows = hh * tq
    qs = [q_ref[c * hh:(c + 1) * hh].reshape(rows, HEAD_DIM) for c in range(nch)]
    qpos = qi * tq + lax.broadcasted_iota(jnp.int32, (tq, 1), 0)
    any_valid = jnp.where(qpos >= CMP_BLK - 1, 1.0, 0.0)

    kc = kc_ref[0, 0]
    blk_end = lax.broadcasted_iota(jnp.int32, (1, nblk), 1) * CMP_BLK + (CMP_BLK - 1)
    valid = blk_end <= qpos
    o_cmp = []
    for c in range(nch):
        s = jnp.where(valid[None], _dot_nt(qs[c], kc).reshape(hh, tq, nblk), NEG)
        e = jnp.exp(s - jnp.max(s, axis=-1, keepdims=True))
        inv = any_valid[None] / jnp.sum(e, axis=-1, keepdims=True)
        o_cmp.append(_dot(e.reshape(rows, nblk).astype(BF16), vc_ref[0, 0]) * inv.reshape(rows, 1))

    qpos_l = qi * tq + lax.broadcasted_iota(jnp.int32, (1, tq), 1)
    blk_end_s = lax.broadcasted_iota(jnp.int32, (nblk, 1), 0) * CMP_BLK + (CMP_BLK - 1)
    valid_t = blk_end_s <= qpos_l
    imp_t = jnp.zeros((nblk, tq), F32)
    for c in range(nch):
        s_t = _dot_nt(kc, qs[c])
        for r in range(hh):
            sr = jnp.where(valid_t, s_t[:, r * tq:(r + 1) * tq], NEG)
            er = jnp.exp(sr - jnp.max(sr, axis=0, keepdims=True))
            imp_t = imp_t + er / jnp.sum(er, axis=0, keepdims=True)
    imp_t = imp_t * jnp.where(qpos_l >= CMP_BLK - 1, 1.0, 0.0)
    sel_t = _select_blocks_t(imp_t, qpos_l // CMP_BLK, nsel)
    pad = e_ref.shape[0] - nblk
    if pad:
        sel_t = jnp.concatenate([sel_t, jnp.zeros((pad, tq), F32)], axis=0)
    sel = sel_t.T.astype(BF16)

    def kv_step(j, carry):
        off = pl.multiple_of(j * kt, kt)
        kpos = off + lax.broadcasted_iota(jnp.int32, (1, kt), 1)
        hit = _dot(sel, e_ref[:, pl.ds(off, kt)])
        bias = jnp.where(kpos <= qpos, (hit - 1.0) * (-NEG), NEG)
        k_t = ks_ref[0, 0, pl.ds(off, kt), :]
        v_t = vs_ref[0, 0, pl.ds(off, kt), :]
        out = []
        for c in range(nch):
            m_i, l_i, acc = carry[3 * c:3 * c + 3]
            sj = _dot_nt(qs[c], k_t).reshape(hh, tq, kt) + bias[None]
            m_new = jnp.maximum(m_i, jnp.max(sj, axis=-1, keepdims=True))
            alpha = jnp.exp(m_i - m_new)
            pj = jnp.exp(sj - m_new)
            l_new = alpha * l_i + jnp.sum(pj, axis=-1, keepdims=True)
            pv = _dot(pj.reshape(rows, kt).astype(BF16), v_t)
            out += [m_new, l_new, acc * alpha.reshape(rows, 1) + pv]
        return tuple(out)

    n_tiles = (qi * tq + tq + kt - 1) // kt
    init = (jnp.full((hh, tq, 1), NEG, F32), jnp.zeros((hh, tq, 1), F32), jnp.zeros((rows, HEAD_DIM), F32)) * nch
    fin = lax.fori_loop(0, n_tiles, kv_step, init)
    o_sel = [fin[3 * c + 2] / fin[3 * c + 1].reshape(rows, 1) for c in range(nch)]

    span = WINDOW + tq
    start = pl.multiple_of(jnp.maximum(qi * tq - WINDOW, 0), tq)
    kpos = start + lax.broadcasted_iota(jnp.int32, (1, span), 1)
    band = jnp.logical_and(kpos <= qpos, kpos > qpos - WINDOW)
    k_w = kw_ref[0, 0, pl.ds(start, span), :]
    v_w = vw_ref[0, 0, pl.ds(start, span), :]
    o_win = []
    for c in range(nch):
        sw = jnp.where(band[None], _dot_nt(qs[c], k_w).reshape(hh, tq, span), NEG)
        ew = jnp.exp(sw - jnp.max(sw, axis=-1, keepdims=True))
        inv_w = 1.0 / jnp.sum(ew, axis=-1, keepdims=True)
        o_win.append(_dot(ew.reshape(rows, span).astype(BF16), v_w) * inv_w.reshape(rows, 1))

    for c in range(nch):
        for r in range(hh):
            sl = slice(r * tq, (r + 1) * tq)
            cs = slice((c * hh + r) * HEAD_DIM, (c * hh + r + 1) * HEAD_DIM)
            o_ref[0, :, cs] = o_cmp[c][sl].astype(BF16)
            o_ref[1, :, cs] = o_sel[c][sl].astype(BF16)
            o_ref[2, :, cs] = o_win[c][sl].astype(BF16)


def _attn_prompt(q_hm, kcmp, vcmp, kv_hm, expand, nbatch, seq, tq, kt):
    nq = seq // tq
    nblk = seq // CMP_BLK
    kv = lambda seg: pl.BlockSpec((1, 1, seq, HEAD_DIM), lambda b, g, i: (seg, g, b, 0))
    cmp = pl.BlockSpec((1, 1, nblk, HEAD_DIM), lambda b, g, i: (b, g, 0, 0))
    return pl.pallas_call(
        functools.partial(_attn_prompt_kernel, kt=kt, nsel=min(N_SEL, nblk)),
        grid=(nbatch, N_KV, nq),
        in_specs=[pl.BlockSpec((GQA_R, tq, HEAD_DIM), lambda b, g, i: (g, b * nq + i, 0)),
                  cmp, cmp, kv(2), kv(3), kv(4), kv(5),
                  pl.BlockSpec(expand.shape, lambda b, g, i: (0, 0))],
        out_specs=pl.BlockSpec((3, tq, GQA_R * HEAD_DIM), lambda b, g, i: (0, b * nq + i, g)),
        out_shape=jax.ShapeDtypeStruct((3, nbatch * seq, D_ATT), BF16),
        compiler_params=_cparams(("parallel", "parallel", "arbitrary")),
        name="attn_prompt",
    )(q_hm, kcmp, vcmp, kv_hm, kv_hm, kv_hm, kv_hm, expand)


def _attn_sample_kernel(pt_ref, ks_hbm, vs_hbm, q_ref, kc_ref, vc_ref, ksn_ref, vsn_ref, kwc_ref, vwc_ref,
                        kwn_ref, vwn_ref, e_ref, o_ref, okw_ref, ovw_ref, kbuf, vbuf, newbuf, ksem, vsem,
                        *, n_pages, n_tok, nsel):
    b = pl.program_id(0)
    slot = _gather_step(pt_ref, (ks_hbm, vs_hbm), (kbuf, vbuf), (ksem, vsem), n_pages)
    past = n_pages * PAGE_SIZE
    nblk_c = past // CMP_BLK
    nblk = kc_ref.shape[2]
    rows = GQA_R * n_tok
    lw = kwc_ref.shape[0] // N_KV
    nr = n_tok * N_KV

    @pl.when(b == 0)
    def _():
        newbuf[...] = jnp.zeros(newbuf.shape, F32)

    for i, ref in enumerate((ksn_ref, vsn_ref, kwn_ref, vwn_ref)):
        newbuf[i, 0:nr, :] = ref[...]

    okw_ref[0:(lw - n_tok) * N_KV, :] = kwc_ref[nr:lw * N_KV, :]
    okw_ref[(lw - n_tok) * N_KV:lw * N_KV, :] = kwn_ref[...]
    ovw_ref[0:(lw - n_tok) * N_KV, :] = vwc_ref[nr:lw * N_KV, :]
    ovw_ref[(lw - n_tok) * N_KV:lw * N_KV, :] = vwn_ref[...]

    rows_all = N_KV * rows
    t_row = lax.broadcasted_iota(jnp.int32, (rows_all, 1), 0) % n_tok
    qpos = past + t_row
    nidx = lax.broadcasted_iota(jnp.int32, (rows_all, nblk), 1)
    valid = nidx * CMP_BLK + (CMP_BLK - 1) <= qpos
    cur = qpos // CMP_BLK
    forced = jnp.logical_or(jnp.logical_or(nidx == cur, nidx == cur - 1), nidx == 0)
    new_lane = lax.broadcasted_iota(jnp.int32, (1, PAGE_SIZE), 1)
    new_ok = jnp.logical_and(new_lane < n_tok, past + new_lane <= qpos)
    wpos = (past - lw) + lax.broadcasted_iota(jnp.int32, (1, lw), 1)
    win_ok = jnp.logical_and(wpos <= qpos, wpos > qpos - WINDOW)
    win_new_ok = jnp.logical_and(new_ok, past + new_lane > qpos - WINDOW)

    groups = range(N_KV)
    grp = lambda x, g: x[g * rows:(g + 1) * rows]
    stack = lambda f: jnp.concatenate([f(g) for g in groups], axis=0)
    group_rows = lambda ref, n: (lambda g: ref[pl.ds(g, n, stride=N_KV), :].astype(BF16))
    new_rows = lambda i: (lambda g: newbuf[i, pl.ds(g, PAGE_SIZE, stride=N_KV), :].astype(BF16))
    qs = [q_ref[g, 0].astype(BF16) for g in groups]

    def attend(s_parts, v_parts):
        ps = _softmax_lanes(jnp.concatenate(s_parts, axis=1)).astype(BF16)
        outs = []
        for g in groups:
            pg, off, acc = grp(ps, g), 0, None
            for sp, vp in zip(s_parts, v_parts):
                d = _dot(pg[:, off:off + sp.shape[1]], vp(g))
                acc = d if acc is None else acc + d
                off += sp.shape[1]
            outs.append(acc)
        return outs

    p = _softmax_lanes(jnp.where(valid, stack(lambda g: _dot_nt(qs[g], kc_ref[0, g].astype(BF16))), NEG))
    p = p * jnp.where(qpos >= CMP_BLK - 1, 1.0, 0.0)
    o_cmp = [_dot(grp(p, g).astype(BF16), vc_ref[0, g].astype(BF16)) for g in groups]

    def head_sum(pg):
        u = pg + pltpu.roll(pg, (GQA_R // 2) * n_tok, 0)
        return u + pltpu.roll(u, n_tok, 0)

    imp = stack(lambda g: head_sum(grp(p, g)))
    imp = jnp.where(forced, 1e4, imp)
    imp = jnp.where(nidx > cur, -1.0, imp)
    rank = jnp.zeros((rows_all, nblk), F32)
    for i in range(nblk):
        col = imp[:, i:i + 1]
        rank = rank + jnp.where(nidx > i, jnp.where(col >= imp, 1.0, 0.0), jnp.where(col > imp, 1.0, 0.0))
    sel = jnp.where(rank < float(nsel), 1.0, 0.0)

    hit = _dot(sel.astype(BF16), e_ref[...])
    k_cached = lambda g: kbuf[slot, pl.ds(g, past, stride=N_KV), :].astype(BF16)
    v_cached = lambda g: vbuf[slot, pl.ds(g, past, stride=N_KV), :].astype(BF16)
    s_c = jnp.where(hit > 0.5, stack(lambda g: _dot_nt(qs[g], k_cached(g))), NEG)
    new_sel = jnp.logical_and(new_ok, sel[:, nblk_c:nblk_c + 1] > 0.5)
    s_n = jnp.where(new_sel, stack(lambda g: _dot_nt(qs[g], new_rows(0)(g))), NEG)
    o_sel = attend([s_c, s_n], [v_cached, new_rows(1)])

    s_w = jnp.where(win_ok, stack(lambda g: _dot_nt(qs[g], group_rows(kwc_ref, lw)(g))), NEG)
    s_wn = jnp.where(win_new_ok, stack(lambda g: _dot_nt(qs[g], new_rows(2)(g))), NEG)
    o_win = attend([s_w, s_wn], [group_rows(vwc_ref, lw), new_rows(3)])

    for j, branch in enumerate((o_cmp, o_sel, o_win)):
        for g in groups:
            for r in range(GQA_R):
                h = g * GQA_R + r
                for t in range(n_tok):
                    o_ref[j, 0, t:t + 1, h * HEAD_DIM:(h + 1) * HEAD_DIM] = branch[g][r * n_tok + t:r * n_tok + t + 1, :]


def _attn_sample(page_table, cache_ks, cache_vs, q_s, kcmp, vcmp, ks_new, vs_new, cache_kw, cache_vw, kw_new, vw_new,
                 expand):
    nb, n_pages = page_table.shape
    past = n_pages * PAGE_SIZE
    n_tok = ks_new.shape[0] // (nb * N_KV)
    rows = GQA_R * n_tok
    nblk = kcmp.shape[2]
    lw = cache_kw.shape[0] // (nb * N_KV)
    new = pl.BlockSpec((n_tok * N_KV, HEAD_DIM), lambda b, pt: (b, 0))
    cmp = pl.BlockSpec((1, N_KV, nblk, HEAD_DIM), lambda b, pt: (b, 0, 0, 0))
    win = pl.BlockSpec((lw * N_KV, HEAD_DIM), lambda b, pt: (b, 0))
    any_ = pl.BlockSpec(memory_space=pl.ANY)
    return pl.pallas_call(
        functools.partial(_attn_sample_kernel, n_pages=n_pages, n_tok=n_tok, nsel=min(N_SEL, past // CMP_BLK + 1)),
        grid_spec=pltpu.PrefetchScalarGridSpec(
            num_scalar_prefetch=1,
            grid=(nb,),
            in_specs=[any_, any_,
                      pl.BlockSpec((N_KV, 1, rows, HEAD_DIM), lambda b, pt: (0, b, 0, 0)),
                      cmp, cmp, new, new, win, win, new, new,
                      pl.BlockSpec(expand.shape, lambda b, pt: (0, 0))],
            out_specs=[pl.BlockSpec((3, 1, n_tok, D_ATT), lambda b, pt: (0, b, 0, 0)), win, win],
            scratch_shapes=[pltpu.VMEM((2, past * N_KV, HEAD_DIM), F32), pltpu.VMEM((2, past * N_KV, HEAD_DIM), F32),
                            pltpu.VMEM((4, PAGE_SIZE * N_KV, HEAD_DIM), F32),
                            pltpu.SemaphoreType.DMA((2,)), pltpu.SemaphoreType.DMA((2,))]),
        out_shape=[jax.ShapeDtypeStruct((3, nb, n_tok, D_ATT), F32),
                   jax.ShapeDtypeStruct(cache_kw.shape, F32), jax.ShapeDtypeStruct(cache_vw.shape, F32)],
        compiler_params=_cparams(("arbitrary",)),
        name="attn_sample",
    )(page_table, cache_ks, cache_vs, q_s, kcmp, vcmp, ks_new, vs_new, cache_kw, cache_vw, kw_new, vw_new, expand)


def _expand_matrix(rows, nkeys):
    return (jnp.arange(rows, dtype=jnp.int32)[:, None] == (jnp.arange(nkeys, dtype=jnp.int32) // CMP_BLK)[None, :]).astype(BF16)


def _pick(m, pref):
    while m % pref:
        pref //= 2
    return pref


def kernel(x_prompt, x_sample, cache_k_cmp, cache_v_cmp, cache_k_sel, cache_v_sel, cache_k_win, cache_v_win, state_h, state_conv, page_table, norm_mix_pre, w_in, conv_w, conv_b, w_ga, b_ga, w_gx, b_gx, lam, pe_k, pe_v, w_phi_k, w_phi_v, norm_rnn_out, norm_att_out, w_out, norm_mix_post, norm_mlp_pre, w_up, w_down, norm_mlp_post):
    depth = w_in.shape[0]
    assert depth == 1, "single layer only"
    bp, seq, d_model = x_prompt.shape
    bs, n_tok, _ = x_sample.shape
    d_rnn = conv_w.shape[-1]
    n_pages = page_table.shape[1]
    past = n_pages * PAGE_SIZE
    lw_in = cache_k_win.shape[2]
    assert lw_in == WINDOW and past % CMP_BLK == 0 and n_tok <= CMP_BLK and d_rnn == RNN_BLOCKS * RNN_BW
    l = 0

    w = w_in[l].astype(BF16)
    c0 = 2 * d_rnn
    q_blk0 = c0 // (GQA_R * HEAD_DIM)
    kv_blk0 = (c0 + D_ATT) // D_KV
    w_gl = jnp.pad(w[:, c0 + D_ATT + 6 * D_KV:], ((0, 0), (0, LANES - 3 * N_HEADS)))
    w_o = w_out[l].astype(BF16)
    w_u = w_up[l].astype(BF16)
    w_d = w_down[l].astype(BF16)
    wg = jnp.concatenate([w_ga[l], w_gx[l]], axis=-1).astype(BF16)
    wk_phi = w_phi_k[l].astype(BF16)
    wv_phi = w_phi_v[l].astype(BF16)
    vec = lambda a: a[l].reshape(1, -1)
    rnn_p = (conv_w[l], vec(conv_b), wg, vec(b_ga), vec(b_gx), vec(lam))

    def mixer_tail(x2d, o_rnn, o3, gates, tm):
        cat = _cat_norm(o_rnn, o3, gates, vec(norm_rnn_out), vec(norm_att_out), _pick(x2d.shape[0], 256))
        mix = _matmul(cat, w_o, tm, 512, name="out_proj")
        h, hn = _post_norm(mix, x2d, vec(norm_mix_post), vec(norm_mlp_pre), _pick(x2d.shape[0], 256), True)
        ff = _mlp(hn, w_u, w_d, _pick(x2d.shape[0], 512), 512)
        (y,) = _post_norm(ff, h, vec(norm_mlp_post), vec(norm_mlp_post), _pick(x2d.shape[0], 256), False)
        return y

    mp = bp * seq
    xp2 = x_prompt.reshape(mp, d_model)
    tm_p = _pick(seq, 1024)
    cos_p, sin_p = _rope_tables(jnp.arange(seq, dtype=jnp.int32))
    xn = _norm_cast(xp2, vec(norm_mix_pre), _pick(mp, 512))
    xy = _matmul(xn, w, tm_p, 512, name="proj_xy", n=c0)
    q_hm = _proj_q_prompt(xn, w, q_blk0, cos_p, sin_p, tm_p)
    *kv_nat, kv_hm = _proj_kv(xn, w, kv_blk0, cos_p, sin_p, _pick(seq, 512), True)
    gates = _matmul(xn, w_gl, tm_p, LANES, body=_mm_gate_kernel, name="proj_gate")
    o_rnn, h_p, cv_p = _rglru_prompt(xy, *rnn_p, bp, seq, _pick(seq, 128))
    nblk_p = seq // CMP_BLK
    cos_e, sin_e = _rope_tables(jnp.arange(nblk_p, dtype=jnp.int32) * CMP_BLK + CMP_BLK - 1)
    cmp_rows = _pick(seq, 1024)
    kcmp, vcmp = _cmp_prompt(kv_nat[0], kv_nat[1], pe_k[l], pe_v[l], wk_phi, wv_phi, cos_e, sin_e, bp, seq, cmp_rows)
    e_rows = max(LANES, nblk_p)
    o3 = _attn_prompt(q_hm, kcmp, vcmp, kv_hm, _expand_matrix(e_rows, seq), bp, seq, 128, _pick(seq, 512))
    y_p = mixer_tail(xp2, o_rnn, o3, gates, tm_p).reshape(bp, seq, d_model)
    lw_p = min(WINDOW, seq)
    nat5 = lambda a: a.reshape(1, bp, seq, N_KV, HEAD_DIM)
    p_out = (nat5(kv_nat[0]), nat5(kv_nat[1]), nat5(kv_nat[2]), nat5(kv_nat[3]),
             nat5(kv_nat[4])[:, :, seq - lw_p:], nat5(kv_nat[5])[:, :, seq - lw_p:],
             h_p.reshape(1, bp, d_rnn), cv_p.reshape(1, bp, CONV_W - 1, d_rnn))

    ms = bs * n_tok
    xs_tb = x_sample.swapaxes(0, 1).reshape(ms, d_model)
    pos_s = past + jnp.arange(n_tok, dtype=jnp.int32)
    cos_s, sin_s = _rope_tables(jnp.repeat(pos_s, bs))
    xn_s = _norm_cast(xs_tb, vec(norm_mix_pre), ms)
    xy_s = _matmul(xn_s, w, ms, 512, name="proj_xy", n=c0)
    q_s = _proj_q_sample(xn_s, w, q_blk0, cos_s, sin_s, n_tok)
    kv_s = _proj_kv(xn_s, w, kv_blk0, cos_s, sin_s, ms, False)
    gates_s = _matmul(xn_s, w_gl, ms, LANES, body=_mm_gate_kernel, name="proj_gate")
    o_rnn_s, h_s, cv_s = _rglru_sample(xy_s, state_conv[l].swapaxes(0, 1), state_h[l], *rnn_p, n_tok, past)
    kv_bt = [a.reshape(n_tok, bs, N_KV, HEAD_DIM).swapaxes(0, 1) for a in kv_s]
    kv_s2 = [a.reshape(ms * N_KV, HEAD_DIM) for a in kv_bt]
    rows2d = lambda c: c.reshape(-1, HEAD_DIM)
    nblk_c = past // CMP_BLK
    npad = 8
    cos_es, sin_es = _rope_tables(jnp.arange(nblk_c + npad, dtype=jnp.int32) * CMP_BLK + CMP_BLK - 1)
    kcmp_s, vcmp_s = _cmp_sample(page_table, rows2d(cache_k_cmp), rows2d(cache_v_cmp), kv_s2[0], kv_s2[1],
                                 pe_k[l], pe_v[l], wk_phi, wv_phi, cos_es, sin_es, npad)
    o3_s, kw_out, vw_out = _attn_sample(page_table, rows2d(cache_k_sel), rows2d(cache_v_sel), q_s, kcmp_s, vcmp_s,
                                        kv_s2[2], kv_s2[3], rows2d(cache_k_win), rows2d(cache_v_win), kv_s2[4], kv_s2[5],
                                        _expand_matrix(nblk_c + npad, past))
    gates_bt = gates_s.reshape(n_tok, bs, LANES).swapaxes(0, 1).reshape(ms, LANES)
    y_s = mixer_tail(x_sample.reshape(ms, d_model), o_rnn_s.reshape(ms, d_rnn), o3_s.reshape(3, ms, D_ATT), gates_bt,
                     ms).reshape(bs, n_tok, d_model)
    new5 = lambda a: a.reshape(1, bs, n_tok, N_KV, HEAD_DIM)
    s_out = (new5(kv_bt[0]), new5(kv_bt[1]), new5(kv_bt[2]), new5(kv_bt[3]),
             kw_out.reshape(cache_k_win.shape), vw_out.reshape(cache_v_win.shape),
             h_s.reshape(1, bs, d_rnn), cv_s.swapaxes(0, 1).reshape(1, bs, CONV_W - 1, d_rnn))
    return (y_p, y_s) + p_out + s_out
```

```python
import functools

import jax
import jax.numpy as jnp
import numpy as np
from jax import lax
from jax.experimental import pallas as pl
from jax.experimental.pallas import tpu as pltpu

F32 = jnp.float32
BF16 = jnp.bfloat16

HEAD_DIM = 128
N_KV = 4
GQA_R = 4
N_HEADS = N_KV * GQA_R
D_ATT = N_HEADS * HEAD_DIM
D_KV = N_KV * HEAD_DIM
CMP_BLK = 64
N_SEL = 16
WINDOW = 512
CONV_W = 4
C_GATE = 8.0
RNN_BLOCKS = 16
RNN_BW = 128
ROPE_THETA = 10000.0
EPS = 1e-6
NEG = -1e30
PAGE_SIZE = 128
SCALE = HEAD_DIM ** -0.5
LOG2E = 1.4426950408889634
QSCALE = SCALE * LOG2E
LANES = 128
VMEM_LIMIT = 56 * 1024 * 1024


def _cparams(sem):
    return pltpu.CompilerParams(dimension_semantics=sem, vmem_limit_bytes=VMEM_LIMIT)


def _rope_tables(pos):
    half = HEAD_DIM // 2
    inv = ROPE_THETA ** (-jnp.arange(half, dtype=F32) * (2.0 / HEAD_DIM))
    ang = pos.astype(F32)[:, None] * inv[None, :]
    c, s = jnp.cos(ang), jnp.sin(ang)
    return jnp.concatenate([c, c], axis=-1), jnp.concatenate([-s, s], axis=-1)


def _rope(x, cos2, sin2):
    return x * cos2 + pltpu.roll(x, HEAD_DIM // 2, 1) * sin2


def _rms(x):
    return x * lax.rsqrt(jnp.mean(x * x, axis=-1, keepdims=True) + EPS)


def _gelu(x):
    return 0.5 * x * (1.0 + jnp.tanh(0.7978845608028654 * (x + 0.044715 * (x * x * x))))


def _softplus(x):
    return jnp.maximum(x, 0.0) + jnp.log1p(jnp.exp(-jnp.abs(x)))


def _sigmoid(x):
    return 1.0 / (1.0 + jnp.exp(-x))


def _norm_kernel(x_ref, g_ref, o_ref):
    o_ref[...] = (_rms(x_ref[...]) * g_ref[...]).astype(o_ref.dtype)


def _norm_cast(x, g, tm):
    m, d = x.shape
    return pl.pallas_call(
        _norm_kernel,
        grid=(m // tm,),
        in_specs=[pl.BlockSpec((tm, d), lambda i: (i, 0)), pl.BlockSpec((1, d), lambda i: (0, 0))],
        out_specs=pl.BlockSpec((tm, d), lambda i: (i, 0)),
        out_shape=jax.ShapeDtypeStruct((m, d), BF16),
        compiler_params=_cparams(("parallel",)),
        name="norm_cast",
    )(x, g)


def _post_kernel(acc_ref, res_ref, g_ref, g2_ref, y_ref, *hn_ref):
    y = res_ref[...] + _rms(acc_ref[...]) * g_ref[...]
    y_ref[...] = y
    if hn_ref:
        hn_ref[0][...] = (_rms(y) * g2_ref[...]).astype(BF16)


def _post_norm(acc, res, g, g2, tm, with_hn):
    m, d = acc.shape
    row = pl.BlockSpec((tm, d), lambda i: (i, 0))
    vec = pl.BlockSpec((1, d), lambda i: (0, 0))
    out_shape = [jax.ShapeDtypeStruct((m, d), F32)]
    out_specs = [row]
    if with_hn:
        out_shape.append(jax.ShapeDtypeStruct((m, d), BF16))
        out_specs.append(row)
    return pl.pallas_call(
        _post_kernel,
        grid=(m // tm,),
        in_specs=[row, row, vec, vec],
        out_specs=out_specs,
        out_shape=out_shape,
        compiler_params=_cparams(("parallel",)),
        name="post_norm",
    )(acc, res, g, g2)


def _cat_norm_kernel(orn_ref, o3_ref, gate_ref, grn_ref, gat_ref, cat_ref, tmp):
    d = orn_ref.shape[1]
    cat_ref[:, :d] = (_rms(orn_ref[...].astype(F32)) * grn_ref[...]).astype(BF16)
    g = gate_ref[...]
    ss = jnp.zeros((g.shape[0], 1), F32)
    for h in range(N_HEADS):
        sl = slice(h * HEAD_DIM, (h + 1) * HEAD_DIM)
        oh = (g[:, 3 * h:3 * h + 1] * o3_ref[0, :, sl].astype(F32)
              + g[:, 3 * h + 1:3 * h + 2] * o3_ref[1, :, sl].astype(F32)
              + g[:, 3 * h + 2:3 * h + 3] * o3_ref[2, :, sl].astype(F32))
        tmp[:, sl] = oh
        ss = ss + jnp.sum(oh * oh, axis=-1, keepdims=True)
    inv = lax.rsqrt(ss * (1.0 / D_ATT) + EPS)
    cat_ref[:, d:] = (tmp[...] * inv * gat_ref[...]).astype(BF16)


def _cat_norm(o_rnn, o3, gates, g_rnn, g_att, tm):
    m, d = o_rnn.shape
    return pl.pallas_call(
        _cat_norm_kernel,
        grid=(m // tm,),
        in_specs=[pl.BlockSpec((tm, d), lambda i: (i, 0)),
                  pl.BlockSpec((3, tm, D_ATT), lambda i: (0, i, 0)),
                  pl.BlockSpec((tm, LANES), lambda i: (i, 0)),
                  pl.BlockSpec((1, d), lambda i: (0, 0)),
                  pl.BlockSpec((1, D_ATT), lambda i: (0, 0))],
        out_specs=pl.BlockSpec((tm, d + D_ATT), lambda i: (i, 0)),
        out_shape=jax.ShapeDtypeStruct((m, d + D_ATT), BF16),
        scratch_shapes=[pltpu.VMEM((tm, D_ATT), F32)],
        compiler_params=_cparams(("parallel",)),
        name="cat_norm",
    )(o_rnn, o3, gates, g_rnn, g_att)


def _dot(a, b):
    return jnp.dot(a, b, preferred_element_type=F32)


def _dot_nt(a, b):
    return lax.dot_general(a, b, (((1,), (1,)), ((), ())), preferred_element_type=F32)


def _mm_plain_kernel(a_ref, b_ref, o_ref):
    o_ref[...] = _dot(a_ref[...], b_ref[...])


def _mm_gate_kernel(a_ref, b_ref, o_ref):
    o_ref[...] = _sigmoid(_dot(a_ref[...], b_ref[...]))


def _matmul(a, b, tm, tn, body=_mm_plain_kernel, name="matmul", n=None, col0=0):
    m, k = a.shape
    n = b.shape[1] if n is None else n
    return pl.pallas_call(
        body,
        grid=(m // tm, n // tn),
        in_specs=[pl.BlockSpec((tm, k), lambda i, j: (i, 0)), pl.BlockSpec((k, tn), lambda i, j: (0, col0 + j))],
        out_specs=pl.BlockSpec((tm, tn), lambda i, j: (i, j)),
        out_shape=jax.ShapeDtypeStruct((m, n), F32),
        compiler_params=_cparams(("parallel", "arbitrary")),
        name=name,
    )(a, b)


def _mm_q_prompt_kernel(a_ref, b_ref, cos_ref, sin_ref, o_ref):
    acc = _dot(a_ref[...], b_ref[...])
    cos2, sin2 = cos_ref[...], sin_ref[...]
    for r in range(GQA_R):
        x = acc[:, r * HEAD_DIM:(r + 1) * HEAD_DIM]
        o_ref[r] = (_rope(x, cos2, sin2) * QSCALE).astype(BF16)


def _proj_q_prompt(a, w, col0, cos2, sin2, tm):
    m, k = a.shape
    nt = cos2.shape[0] // tm
    return pl.pallas_call(
        _mm_q_prompt_kernel,
        grid=(m // tm, N_KV),
        in_specs=[pl.BlockSpec((tm, k), lambda i, j: (i, 0)),
                  pl.BlockSpec((k, GQA_R * HEAD_DIM), lambda i, j: (0, col0 + j)),
                  pl.BlockSpec((tm, HEAD_DIM), lambda i, j: (i % nt, 0)),
                  pl.BlockSpec((tm, HEAD_DIM), lambda i, j: (i % nt, 0))],
        out_specs=pl.BlockSpec((GQA_R, tm, HEAD_DIM), lambda i, j: (j, i, 0)),
        out_shape=jax.ShapeDtypeStruct((N_HEADS, m, HEAD_DIM), BF16),
        compiler_params=_cparams(("parallel", "arbitrary")),
        name="proj_q_prompt",
    )(a, w, cos2, sin2)


def _mm_q_sample_kernel(a_ref, b_ref, cos_ref, sin_ref, o_ref, tmp, *, n_tok):
    acc = _dot(a_ref[...], b_ref[...])
    cos2, sin2 = cos_ref[...], sin_ref[...]
    nb = o_ref.shape[1]
    for r in range(GQA_R):
        x = acc[:, r * HEAD_DIM:(r + 1) * HEAD_DIM]
        tmp[...] = _rope(x, cos2, sin2) * QSCALE
        for t in range(n_tok):
            o_ref[0, :, n_tok * r + t, :] = tmp[t * nb:(t + 1) * nb, :]


def _proj_q_sample(a, w, col0, cos2, sin2, n_tok):
    m, k = a.shape
    nb = m // n_tok
    return pl.pallas_call(
        functools.partial(_mm_q_sample_kernel, n_tok=n_tok),
        grid=(N_KV,),
        in_specs=[pl.BlockSpec((m, k), lambda j: (0, 0)),
                  pl.BlockSpec((k, GQA_R * HEAD_DIM), lambda j: (0, col0 + j)),
                  pl.BlockSpec((m, HEAD_DIM), lambda j: (0, 0)),
                  pl.BlockSpec((m, HEAD_DIM), lambda j: (0, 0))],
        out_specs=pl.BlockSpec((1, nb, GQA_R * n_tok, HEAD_DIM), lambda j: (j, 0, 0, 0)),
        out_shape=jax.ShapeDtypeStruct((N_KV, nb, GQA_R * n_tok, HEAD_DIM), F32),
        scratch_shapes=[pltpu.VMEM((m, HEAD_DIM), F32)],
        compiler_params=_cparams(("arbitrary",)),
        name="proj_q_sample",
    )(a, w, cos2, sin2)


def _mm_kv_kernel(a_ref, b_ref, cos_ref, sin_ref, *outs, with_hm):
    j = pl.program_id(1)
    tm = a_ref.shape[0]
    acc = _dot(a_ref[...], b_ref[...])
    is_rope = jnp.logical_or(j == 2, j == 4)
    cos2 = jnp.where(is_rope, cos_ref[...], 1.0)
    sin2 = jnp.where(is_rope, sin_ref[...], 0.0)
    ys = [_rope(acc[:, g * HEAD_DIM:(g + 1) * HEAD_DIM], cos2, sin2) for g in range(N_KV)]
    for s in range(6):
        @pl.when(j == s)
        def _(s=s):
            for g in range(N_KV):
                outs[s][pl.ds(g, tm, stride=N_KV), :] = ys[g]
    if with_hm:
        for g in range(N_KV):
            outs[6][0, g] = ys[g].astype(BF16)


def _proj_kv(a, w, col0, cos2, sin2, tm, with_hm):
    m, k = a.shape
    nt = cos2.shape[0] // tm
    nat = pl.BlockSpec((tm * N_KV, HEAD_DIM), lambda i, j: (i, 0))
    out_specs = [nat] * 6
    out_shape = [jax.ShapeDtypeStruct((m * N_KV, HEAD_DIM), F32)] * 6
    if with_hm:
        out_specs = out_specs + [pl.BlockSpec((1, N_KV, tm, HEAD_DIM), lambda i, j: (j, 0, i, 0))]
        out_shape = out_shape + [jax.ShapeDtypeStruct((6, N_KV, m, HEAD_DIM), BF16)]
    return pl.pallas_call(
        functools.partial(_mm_kv_kernel, with_hm=with_hm),
        grid=(m // tm, 6),
        in_specs=[pl.BlockSpec((tm, k), lambda i, j: (i, 0)),
                  pl.BlockSpec((k, D_KV), lambda i, j: (0, col0 + j)),
                  pl.BlockSpec((tm, HEAD_DIM), lambda i, j: (i % nt, 0)),
                  pl.BlockSpec((tm, HEAD_DIM), lambda i, j: (i % nt, 0))],
        out_specs=out_specs,
        out_shape=out_shape,
        compiler_params=_cparams(("parallel", "arbitrary")),
        name="proj_kv",
    )(a, w, cos2, sin2)


def _mlp_kernel(h_ref, wu_ref, wd_ref, o_ref):
    @pl.when(pl.program_id(1) == 0)
    def _():
        o_ref[...] = jnp.zeros(o_ref.shape, F32)

    u = _dot(h_ref[...], wu_ref[...])
    f = jnp.square(jnp.maximum(u, 0.0)).astype(BF16)
    o_ref[...] += _dot(f, wd_ref[...])


def _mlp(hn, w_up, w_down, tm, tf):
    m, d = hn.shape
    f = w_up.shape[1]
    return pl.pallas_call(
        _mlp_kernel,
        grid=(m // tm, f // tf),
        in_specs=[pl.BlockSpec((tm, d), lambda i, j: (i, 0)),
                  pl.BlockSpec((d, tf), lambda i, j: (0, j)),
                  pl.BlockSpec((tf, d), lambda i, j: (j, 0))],
        out_specs=pl.BlockSpec((tm, d), lambda i, j: (i, 0)),
        out_shape=jax.ShapeDtypeStruct((m, d), F32),
        compiler_params=_cparams(("parallel", "arbitrary")),
        name="mlp",
    )(hn, w_up, w_down)


def _rglru_gates(xc, gg, bga, bgx, lam, first_row):
    ra = _sigmoid(gg[:, :RNN_BW] + bga)
    rx = _sigmoid(gg[:, RNN_BW:] + bgx)
    log_a = (-C_GATE) * ra * _softplus(-lam)
    a = jnp.exp(log_a)
    th = jnp.tanh(log_a)
    mult = jnp.sqrt((-2.0) * th / (1.0 - th))
    if first_row is not None:
        mult = jnp.where(first_row, 1.0, mult)
    return a, mult * rx * xc


def _rglru_prompt_kernel(xr_ref, yr_ref, cw_ref, cb_ref, wg_ref, bga_ref, bgx_ref, lam_ref,
                         o_ref, h_ref, conv_ref, xbuf, hcar):
    ti = pl.program_id(1)
    tt = xr_ref.shape[0]

    @pl.when(ti == 0)
    def _():
        xbuf[0:8, :] = jnp.zeros((8, xbuf.shape[1]), F32)
        hcar[...] = jnp.zeros(hcar.shape, F32)

    xbuf[8:8 + tt, :] = xr_ref[...]
    row = lax.broadcasted_iota(jnp.int32, (tt, RNN_BW), 0)
    first_row = jnp.logical_and(row == 0, ti == 0)

    def block(n, carry):
        col = pl.ds(pl.multiple_of(n * RNN_BW, RNN_BW), RNN_BW)
        xc = cb_ref[:, col]
        for k in range(CONV_W):
            xc = xc + xbuf[8 - (CONV_W - 1) + k:8 - (CONV_W - 1) + k + tt, col] * cw_ref[k:k + 1, col]
        gg = _dot(xc.astype(BF16), wg_ref[n])
        a, b = _rglru_gates(xc, gg, bga_ref[:, col], bgx_ref[:, col], lam_ref[:, col], first_row)
        s = 1
        while s < tt:
            keep = row >= s
            a_sh = jnp.where(keep, pltpu.roll(a, s, 0), 1.0)
            b_sh = jnp.where(keep, pltpu.roll(b, s, 0), 0.0)
            b = a * b_sh + b
            a = a * a_sh
            s *= 2
        hs = a * hcar[:, col] + b
        hcar[:, col] = hs[tt - 1:tt, :]
        o_ref[:, col] = (hs * _gelu(yr_ref[:, col])).astype(o_ref.dtype)
        return carry

    lax.fori_loop(0, RNN_BLOCKS, block, 0, unroll=2)
    xbuf[0:8, :] = xbuf[tt:tt + 8, :]

    @pl.when(ti == pl.num_programs(1) - 1)
    def _():
        h_ref[0] = hcar[...]
        conv_ref[0] = xbuf[8 - (CONV_W - 1):8, :]


def _rglru_prompt(xy, conv_w, conv_b, wg, b_ga, b_gx, lam, nbatch, seq, tt):
    d = conv_w.shape[1]
    nt = seq // tt
    vec = pl.BlockSpec((1, d), lambda b, t: (0, 0))
    return pl.pallas_call(
        _rglru_prompt_kernel,
        grid=(nbatch, nt),
        in_specs=[pl.BlockSpec((tt, d), lambda b, t: (b * nt + t, 0)),
                  pl.BlockSpec((tt, d), lambda b, t: (b * nt + t, 1)),
                  pl.BlockSpec((CONV_W, d), lambda b, t: (0, 0)),
                  vec,
                  pl.BlockSpec((RNN_BLOCKS, RNN_BW, 2 * RNN_BW), lambda b, t: (0, 0, 0)),
                  vec, vec, vec],
        out_specs=[pl.BlockSpec((tt, d), lambda b, t: (b * nt + t, 0)),
                   pl.BlockSpec((1, 1, d), lambda b, t: (b, 0, 0)),
                   pl.BlockSpec((1, CONV_W - 1, d), lambda b, t: (b, 0, 0))],
        out_shape=[jax.ShapeDtypeStruct((nbatch * seq, d), BF16),
                   jax.ShapeDtypeStruct((nbatch, 1, d), F32),
                   jax.ShapeDtypeStruct((nbatch, CONV_W - 1, d), F32)],
        scratch_shapes=[pltpu.VMEM((tt + 8, d), F32), pltpu.VMEM((1, d), F32)],
        compiler_params=_cparams(("arbitrary", "arbitrary")),
        name="rglru_prompt",
    )(xy, xy, conv_w, conv_b, wg, b_ga, b_gx, lam)


def _rglru_sample_kernel(xr_ref, yr_ref, cst_ref, h0_ref, cw_ref, cb_ref, wg_ref, bga_ref, bgx_ref, lam_ref,
                         o_ref, h_ref, conv_ref, *, n_tok, pos0):
    nb = h0_ref.shape[0]
    xs = [cst_ref[k] for k in range(CONV_W - 1)] + [xr_ref[t * nb:(t + 1) * nb, :] for t in range(n_tok)]
    h = h0_ref[...]
    for t in range(n_tok):
        xc = cb_ref[...]
        for k in range(CONV_W):
            xc = xc + xs[t + k] * cw_ref[k:k + 1, :]
        gg = _dot(xc.astype(BF16), wg_ref[0])
        a, b = _rglru_gates(xc, gg, bga_ref[...], bgx_ref[...], lam_ref[...], True if pos0 + t == 0 else None)
        h = a * h + b
        o_ref[:, t, :] = h * _gelu(yr_ref[t * nb:(t + 1) * nb, :])
    h_ref[...] = h
    for k in range(CONV_W - 1):
        conv_ref[k] = xs[n_tok + k]


def _rglru_sample(xy, cst, h0, conv_w, conv_b, wg, b_ga, b_gx, lam, n_tok, pos0):
    nb, d = h0.shape
    m = nb * n_tok
    vec = pl.BlockSpec((1, RNN_BW), lambda n: (0, n))
    st = pl.BlockSpec((CONV_W - 1, nb, RNN_BW), lambda n: (0, 0, n))
    return pl.pallas_call(
        functools.partial(_rglru_sample_kernel, n_tok=n_tok, pos0=pos0),
        grid=(RNN_BLOCKS,),
        in_specs=[pl.BlockSpec((m, RNN_BW), lambda n: (0, n)),
                  pl.BlockSpec((m, RNN_BW), lambda n: (0, RNN_BLOCKS + n)),
                  st,
                  pl.BlockSpec((nb, RNN_BW), lambda n: (0, n)),
                  pl.BlockSpec((CONV_W, RNN_BW), lambda n: (0, n)),
                  vec,
                  pl.BlockSpec((1, RNN_BW, 2 * RNN_BW), lambda n: (n, 0, 0)),
                  vec, vec, vec],
        out_specs=[pl.BlockSpec((nb, n_tok, RNN_BW), lambda n: (0, 0, n)),
                   pl.BlockSpec((nb, RNN_BW), lambda n: (0, n)),
                   st],
        out_shape=[jax.ShapeDtypeStruct((nb, n_tok, d), F32),
                   jax.ShapeDtypeStruct((nb, d), F32),
                   jax.ShapeDtypeStruct((CONV_W - 1, nb, d), F32)],
        compiler_params=_cparams(("parallel",)),
        name="rglru_sample",
    )(xy, xy, cst, h0, conv_w, conv_b, wg, b_ga, b_gx, lam)


def _block_summary(x, pe_ref, w_ref, nblk):
    m = jnp.sum(x.reshape(nblk, CMP_BLK, HEAD_DIM), axis=1) * (1.0 / CMP_BLK)
    m = m + jnp.mean(pe_ref[...], axis=0, keepdims=True)
    return _dot(m.astype(BF16), w_ref[...])


def _cmp_prompt_kernel(kc_ref, vc_ref, pek_ref, pev_ref, wk_ref, wv_ref, cos_ref, sin_ref, ko_ref, vo_ref):
    nblk = ko_ref.shape[2]
    rows = nblk * CMP_BLK
    for g in range(N_KV):
        kk = _block_summary(kc_ref[pl.ds(g, rows, stride=N_KV), :], pek_ref, wk_ref, nblk)
        ko_ref[0, g] = _rope(kk, cos_ref[...], sin_ref[...]).astype(BF16)
        vo_ref[0, g] = _block_summary(vc_ref[pl.ds(g, rows, stride=N_KV), :], pev_ref, wv_ref, nblk).astype(BF16)


def _cmp_prompt(kc, vc, pe_k, pe_v, w_k, w_v, cos_e, sin_e, nbatch, seq, rows):
    nblk = rows // CMP_BLK
    nt = seq // rows
    nat = pl.BlockSpec((rows * N_KV, HEAD_DIM), lambda b, t: (b * nt + t, 0))
    full = lambda shape: pl.BlockSpec(shape, lambda b, t: (0,) * len(shape))
    tab = pl.BlockSpec((nblk, HEAD_DIM), lambda b, t: (t, 0))
    out = pl.BlockSpec((1, N_KV, nblk, HEAD_DIM), lambda b, t: (b, 0, t, 0))
    shape = jax.ShapeDtypeStruct((nbatch, N_KV, seq // CMP_BLK, HEAD_DIM), BF16)
    return pl.pallas_call(
        _cmp_prompt_kernel,
        grid=(nbatch, nt),
        in_specs=[nat, nat, full((CMP_BLK, HEAD_DIM)), full((CMP_BLK, HEAD_DIM)),
                  full((HEAD_DIM, HEAD_DIM)), full((HEAD_DIM, HEAD_DIM)), tab, tab],
        out_specs=[out, out],
        out_shape=[shape, shape],
        compiler_params=_cparams(("parallel", "arbitrary")),
        name="cmp_prompt",
    )(kc, vc, pe_k, pe_v, w_k, w_v, cos_e, sin_e)


def _page_copies(pt_ref, hbm, buf, sem, bb, slot, n_pages):
    pr = PAGE_SIZE * N_KV
    return [pltpu.make_async_copy(hbm.at[pl.ds(pl.multiple_of(pt_ref[bb, p] * pr, pr), pr)],
                                  buf.at[slot, pl.ds(p * pr, pr)], sem.at[slot])
            for p in range(n_pages)]


def _gather_step(pt_ref, hbms, bufs, sems, n_pages):
    b = pl.program_id(0)
    slot = b % 2

    @pl.when(b == 0)
    def _():
        for hbm, buf, sem in zip(hbms, bufs, sems):
            for c in _page_copies(pt_ref, hbm, buf, sem, 0, 0, n_pages):
                c.start()

    @pl.when(b + 1 < pl.num_programs(0))
    def _():
        for hbm, buf, sem in zip(hbms, bufs, sems):
            for c in _page_copies(pt_ref, hbm, buf, sem, b + 1, 1 - slot, n_pages):
                c.start()

    for hbm, buf, sem in zip(hbms, bufs, sems):
        for c in _page_copies(pt_ref, hbm, buf, sem, b, slot, n_pages):
            c.wait()
    return slot


def _cmp_sample_kernel(pt_ref, kc_hbm, vc_hbm, kn_ref, vn_ref, pek_ref, pev_ref, wk_ref, wv_ref, cos_ref, sin_ref,
                       ko_ref, vo_ref, kbuf, vbuf, ksem, vsem, *, n_pages):
    slot = _gather_step(pt_ref, (kc_hbm, vc_hbm), (kbuf, vbuf), (ksem, vsem), n_pages)
    past = n_pages * PAGE_SIZE
    nblk = past // CMP_BLK
    npad = ko_ref.shape[2] - nblk
    n_tok = kn_ref.shape[0] // N_KV

    def tail(new_ref, pe_ref, w_ref, g):
        m = (jnp.sum(new_ref[pl.ds(g, n_tok, stride=N_KV), :], axis=0, keepdims=True)
             + jnp.sum(pe_ref[...], axis=0, keepdims=True)) * (1.0 / CMP_BLK)
        return _dot(jnp.broadcast_to(m, (npad, HEAD_DIM)).astype(BF16), w_ref[...])

    for g in range(N_KV):
        kk = _block_summary(kbuf[slot, pl.ds(g, past, stride=N_KV), :], pek_ref, wk_ref, nblk)
        ko_ref[0, g, 0:nblk, :] = _rope(kk, cos_ref[0:nblk, :], sin_ref[0:nblk, :])
        kt = tail(kn_ref, pek_ref, wk_ref, g)
        ko_ref[0, g, nblk:nblk + npad, :] = _rope(kt, cos_ref[nblk:nblk + npad, :], sin_ref[nblk:nblk + npad, :])
        vo_ref[0, g, 0:nblk, :] = _block_summary(vbuf[slot, pl.ds(g, past, stride=N_KV), :], pev_ref, wv_ref, nblk)
        vo_ref[0, g, nblk:nblk + npad, :] = tail(vn_ref, pev_ref, wv_ref, g)


def _cmp_sample(page_table, cache_k, cache_v, k_new, v_new, pe_k, pe_v, w_k, w_v, cos_e, sin_e, npad):
    nb, n_pages = page_table.shape
    past = n_pages * PAGE_SIZE
    nblk = past // CMP_BLK
    n_tok = k_new.shape[0] // (nb * N_KV)
    full = lambda shape: pl.BlockSpec(shape, lambda b, pt: (0,) * len(shape))
    new = pl.BlockSpec((n_tok * N_KV, HEAD_DIM), lambda b, pt: (b, 0))
    out = pl.BlockSpec((1, N_KV, nblk + npad, HEAD_DIM), lambda b, pt: (b, 0, 0, 0))
    shape = jax.ShapeDtypeStruct((nb, N_KV, nblk + npad, HEAD_DIM), F32)
    return pl.pallas_call(
        functools.partial(_cmp_sample_kernel, n_pages=n_pages),
        grid_spec=pltpu.PrefetchScalarGridSpec(
            num_scalar_prefetch=1,
            grid=(nb,),
            in_specs=[pl.BlockSpec(memory_space=pl.ANY), pl.BlockSpec(memory_space=pl.ANY), new, new,
                      full((CMP_BLK, HEAD_DIM)), full((CMP_BLK, HEAD_DIM)),
                      full((HEAD_DIM, HEAD_DIM)), full((HEAD_DIM, HEAD_DIM)),
                      full((nblk + npad, HEAD_DIM)), full((nblk + npad, HEAD_DIM))],
            out_specs=[out, out],
            scratch_shapes=[pltpu.VMEM((2, past * N_KV, HEAD_DIM), F32), pltpu.VMEM((2, past * N_KV, HEAD_DIM), F32),
                            pltpu.SemaphoreType.DMA((2,)), pltpu.SemaphoreType.DMA((2,))]),
        out_shape=[shape, shape],
        compiler_params=_cparams(("arbitrary",)),
        name="cmp_sample",
    )(page_table, cache_k, cache_v, k_new, v_new, pe_k, pe_v, w_k, w_v, cos_e, sin_e)


def _select_blocks_t(imp_t, cur, nsel):
    nblk = imp_t.shape[0]
    nidx = lax.broadcasted_iota(jnp.int32, imp_t.shape, 0)
    forced = jnp.logical_or(jnp.logical_or(nidx == cur, nidx == cur - 1), nidx == 0)
    imp_t = jnp.where(forced, 1e4, imp_t)
    imp_t = jnp.where(nidx > cur, -1.0, imp_t)
    rank = jnp.zeros(imp_t.shape, F32)
    for i in range(nblk):
        row = imp_t[i:i + 1, :]
        rank = rank + jnp.where(nidx > i, jnp.where(row >= imp_t, 1.0, 0.0), jnp.where(row > imp_t, 1.0, 0.0))
    return jnp.where(rank < float(nsel), 1.0, 0.0)


def _softmax_lanes(s):
    m = jnp.max(s, axis=-1, keepdims=True)
    e = jnp.exp2(s - m)
    return e / jnp.sum(e, axis=-1, keepdims=True)


def _attn_prompt_kernel(q_ref, kc_ref, vc_ref, ks_ref, vs_ref, kw_ref, vw_ref, e_ref, o_ref, *, kt, kv_unroll, nsel):
    qi = pl.program_id(2)
    tq = q_ref.shape[1]
    nblk = kc_ref.shape[2]
    hh = GQA_R
    nch = GQA_R // hh
    rows = hh * tq
    qs = [q_ref[c * hh:(c + 1) * hh].reshape(rows, HEAD_DIM) for c in range(nch)]
    qpos = qi * tq + lax.broadcasted_iota(jnp.int32, (tq, 1), 0)
    any_valid = jnp.where(qpos >= CMP_BLK - 1, 1.0, 0.0)

    kc = kc_ref[0, 0]
    blk_end = lax.broadcasted_iota(jnp.int32, (1, nblk), 1) * CMP_BLK + (CMP_BLK - 1)
    valid = blk_end <= qpos
    o_cmp = []
    for c in range(nch):
        s = jnp.where(valid[None], _dot_nt(qs[c], kc).reshape(hh, tq, nblk), NEG)
        e = jnp.exp2(s - jnp.max(s, axis=-1, keepdims=True))
        inv = any_valid[None] / jnp.sum(e, axis=-1, keepdims=True)
        o_cmp.append(_dot(e.reshape(rows, nblk).astype(BF16), vc_ref[0, 0]) * inv.reshape(rows, 1))

    qpos_l = qi * tq + lax.broadcasted_iota(jnp.int32, (1, tq), 1)
    blk_end_s = lax.broadcasted_iota(jnp.int32, (nblk, 1), 0) * CMP_BLK + (CMP_BLK - 1)
    valid_t = blk_end_s <= qpos_l
    imp_t = jnp.zeros((nblk, tq), F32)
    for c in range(nch):
        s_t = _dot_nt(kc, qs[c])
        for r in range(hh):
            sr = jnp.where(valid_t, s_t[:, r * tq:(r + 1) * tq], NEG)
            er = jnp.exp2(sr - jnp.max(sr, axis=0, keepdims=True))
            imp_t = imp_t + er / jnp.sum(er, axis=0, keepdims=True)
    imp_t = imp_t * jnp.where(qpos_l >= CMP_BLK - 1, 1.0, 0.0)
    sel_t = _select_blocks_t(imp_t, qpos_l // CMP_BLK, nsel)
    pad = e_ref.shape[0] - nblk
    if pad:
        sel_t = jnp.concatenate([sel_t, jnp.zeros((pad, tq), F32)], axis=0)
    sel = sel_t.T.astype(BF16)

    def kv_tile(j, carry):
        off = pl.multiple_of(j * kt, kt)
        kpos = off + lax.broadcasted_iota(jnp.int32, (1, kt), 1)
        hit = _dot(sel, e_ref[:, pl.ds(off, kt)])
        bias = jnp.where(kpos <= qpos, (hit - 1.0) * (-NEG), NEG)
        k_t = ks_ref[0, 0, pl.ds(off, kt), :]
        v_t = vs_ref[0, 0, pl.ds(off, kt), :]
        out = []
        for c in range(nch):
            m_i, l_i, acc = carry[3 * c:3 * c + 3]
            sj = _dot_nt(qs[c], k_t).reshape(hh, tq, kt) + bias[None]
            m_new = jnp.maximum(m_i, jnp.max(sj, axis=-1, keepdims=True))
            alpha = jnp.exp2(m_i - m_new)
            pj = jnp.exp2(sj - m_new)
            l_new = alpha * l_i + jnp.sum(pj, axis=-1, keepdims=True)
            pv = _dot(pj.reshape(rows, kt).astype(BF16), v_t)
            out += [m_new, l_new, acc * alpha.reshape(rows, 1) + pv]
        return tuple(out)

    def kv_step(jj, carry):
        for u in range(kv_unroll):
            carry = kv_tile(jj * kv_unroll + u, carry)
        return carry

    n_tiles = (qi * tq + tq + kt - 1) // kt
    init = (jnp.full((hh, tq, 1), NEG, F32), jnp.zeros((hh, tq, 1), F32), jnp.zeros((rows, HEAD_DIM), F32)) * nch
    fin = lax.fori_loop(0, n_tiles // kv_unroll, kv_step, init)
    for u in range(kv_unroll - 1):
        j_left = (n_tiles // kv_unroll) * kv_unroll + u
        fin = lax.cond(j_left < n_tiles, functools.partial(kv_tile, j_left), lambda c: c, fin)
    o_sel = [fin[3 * c + 2] / fin[3 * c + 1].reshape(rows, 1) for c in range(nch)]

    span = WINDOW + tq
    start = pl.multiple_of(jnp.maximum(qi * tq - WINDOW, 0), tq)
    kpos = start + lax.broadcasted_iota(jnp.int32, (1, span), 1)
    band = jnp.logical_and(kpos <= qpos, kpos > qpos - WINDOW)
    k_w = kw_ref[0, 0, pl.ds(start, span), :]
    v_w = vw_ref[0, 0, pl.ds(start, span), :]
    o_win = []
    for c in range(nch):
        sw = jnp.where(band[None], _dot_nt(qs[c], k_w).reshape(hh, tq, span), NEG)
        ew = jnp.exp2(sw - jnp.max(sw, axis=-1, keepdims=True))
        inv_w = 1.0 / jnp.sum(ew, axis=-1, keepdims=True)
        o_win.append(_dot(ew.reshape(rows, span).astype(BF16), v_w) * inv_w.reshape(rows, 1))

    for c in range(nch):
        for r in range(hh):
            sl = slice(r * tq, (r + 1) * tq)
            cs = slice((c * hh + r) * HEAD_DIM, (c * hh + r + 1) * HEAD_DIM)
            o_ref[0, :, cs] = o_cmp[c][sl].astype(BF16)
            o_ref[1, :, cs] = o_sel[c][sl].astype(BF16)
            o_ref[2, :, cs] = o_win[c][sl].astype(BF16)


def _attn_prompt(q_hm, kcmp, vcmp, kv_hm, expand, nbatch, seq, tq, kt):
    nq = seq // tq
    nblk = seq // CMP_BLK
    kv = lambda seg: pl.BlockSpec((1, 1, seq, HEAD_DIM), lambda b, g, i: (seg, g, b, 0))
    cmp = pl.BlockSpec((1, 1, nblk, HEAD_DIM), lambda b, g, i: (b, g, 0, 0))
    return pl.pallas_call(
        functools.partial(_attn_prompt_kernel, kt=kt, kv_unroll=2, nsel=min(N_SEL, nblk)),
        grid=(nbatch, N_KV, nq),
        in_specs=[pl.BlockSpec((GQA_R, tq, HEAD_DIM), lambda b, g, i: (g, b * nq + i, 0)),
                  cmp, cmp, kv(2), kv(3), kv(4), kv(5),
                  pl.BlockSpec(expand.shape, lambda b, g, i: (0, 0))],
        out_specs=pl.BlockSpec((3, tq, GQA_R * HEAD_DIM), lambda b, g, i: (0, b * nq + i, g)),
        out_shape=jax.ShapeDtypeStruct((3, nbatch * seq, D_ATT), BF16),
        compiler_params=_cparams(("parallel", "parallel", "arbitrary")),
        name="attn_prompt",
    )(q_hm, kcmp, vcmp, kv_hm, kv_hm, kv_hm, kv_hm, expand)


def _attn_sample_kernel(pt_ref, ks_hbm, vs_hbm, q_ref, kc_ref, vc_ref, ksn_ref, vsn_ref, kwc_ref, vwc_ref,
                        kwn_ref, vwn_ref, e_ref, o_ref, okw_ref, ovw_ref, kbuf, vbuf, newbuf, ksem, vsem,
                        *, n_pages, n_tok, nsel):
    b = pl.program_id(0)
    slot = _gather_step(pt_ref, (ks_hbm, vs_hbm), (kbuf, vbuf), (ksem, vsem), n_pages)
    past = n_pages * PAGE_SIZE
    nblk_c = past // CMP_BLK
    nblk = kc_ref.shape[2]
    rows = GQA_R * n_tok
    lw = kwc_ref.shape[0] // N_KV
    nr = n_tok * N_KV

    @pl.when(b == 0)
    def _():
        newbuf[...] = jnp.zeros(newbuf.shape, F32)

    for i, ref in enumerate((ksn_ref, vsn_ref, kwn_ref, vwn_ref)):
        newbuf[i, 0:nr, :] = ref[...]

    okw_ref[0:(lw - n_tok) * N_KV, :] = kwc_ref[nr:lw * N_KV, :]
    okw_ref[(lw - n_tok) * N_KV:lw * N_KV, :] = kwn_ref[...]
    ovw_ref[0:(lw - n_tok) * N_KV, :] = vwc_ref[nr:lw * N_KV, :]
    ovw_ref[(lw - n_tok) * N_KV:lw * N_KV, :] = vwn_ref[...]

    rows_all = N_KV * rows
    t_row = lax.broadcasted_iota(jnp.int32, (rows_all, 1), 0) % n_tok
    qpos = past + t_row
    nidx = lax.broadcasted_iota(jnp.int32, (rows_all, nblk), 1)
    valid = nidx * CMP_BLK + (CMP_BLK - 1) <= qpos
    cur = qpos // CMP_BLK
    forced = jnp.logical_or(jnp.logical_or(nidx == cur, nidx == cur - 1), nidx == 0)
    new_lane = lax.broadcasted_iota(jnp.int32, (1, PAGE_SIZE), 1)
    new_ok = jnp.logical_and(new_lane < n_tok, past + new_lane <= qpos)
    wpos = (past - lw) + lax.broadcasted_iota(jnp.int32, (1, lw), 1)
    win_ok = jnp.logical_and(wpos <= qpos, wpos > qpos - WINDOW)
    win_new_ok = jnp.logical_and(new_ok, past + new_lane > qpos - WINDOW)

    groups = range(N_KV)
    grp = lambda x, g: x[g * rows:(g + 1) * rows]
    stack = lambda f: jnp.concatenate([f(g) for g in groups], axis=0)
    group_rows = lambda ref, n: (lambda g: ref[pl.ds(g, n, stride=N_KV), :].astype(BF16))
    new_rows = lambda i: (lambda g: newbuf[i, pl.ds(g, PAGE_SIZE, stride=N_KV), :].astype(BF16))
    qs = [q_ref[g, 0].astype(BF16) for g in groups]

    def attend(s_parts, v_parts):
        ps = _softmax_lanes(jnp.concatenate(s_parts, axis=1)).astype(BF16)
        outs = []
        for g in groups:
            pg, off, acc = grp(ps, g), 0, None
            for sp, vp in zip(s_parts, v_parts):
                d = _dot(pg[:, off:off + sp.shape[1]], vp(g))
                acc = d if acc is None else acc + d
                off += sp.shape[1]
            outs.append(acc)
        return outs

    p = _softmax_lanes(jnp.where(valid, stack(lambda g: _dot_nt(qs[g], kc_ref[0, g].astype(BF16))), NEG))
    p = p * jnp.where(qpos >= CMP_BLK - 1, 1.0, 0.0)
    o_cmp = [_dot(grp(p, g).astype(BF16), vc_ref[0, g].astype(BF16)) for g in groups]

    def head_sum(pg):
        u = pg + pltpu.roll(pg, (GQA_R // 2) * n_tok, 0)
        return u + pltpu.roll(u, n_tok, 0)

    imp = stack(lambda g: head_sum(grp(p, g)))
    imp = jnp.where(forced, 1e4, imp)
    imp = jnp.where(nidx > cur, -1.0, imp)
    rank = jnp.zeros((rows_all, nblk), F32)
    for i in range(nblk):
        col = imp[:, i:i + 1]
        rank = rank + jnp.where(nidx > i, jnp.where(col >= imp, 1.0, 0.0), jnp.where(col > imp, 1.0, 0.0))
    sel = jnp.where(rank < float(nsel), 1.0, 0.0)

    hit = _dot(sel.astype(BF16), e_ref[...])
    k_cached = lambda g: kbuf[slot, pl.ds(g, past, stride=N_KV), :].astype(BF16)
    v_cached = lambda g: vbuf[slot, pl.ds(g, past, stride=N_KV), :].astype(BF16)
    s_c = jnp.where(hit > 0.5, stack(lambda g: _dot_nt(qs[g], k_cached(g))), NEG)
    new_sel = jnp.logical_and(new_ok, sel[:, nblk_c:nblk_c + 1] > 0.5)
    s_n = jnp.where(new_sel, stack(lambda g: _dot_nt(qs[g], new_rows(0)(g))), NEG)
    o_sel = attend([s_c, s_n], [v_cached, new_rows(1)])

    s_w = jnp.where(win_ok, stack(lambda g: _dot_nt(qs[g], group_rows(kwc_ref, lw)(g))), NEG)
    s_wn = jnp.where(win_new_ok, stack(lambda g: _dot_nt(qs[g], new_rows(2)(g))), NEG)
    o_win = attend([s_w, s_wn], [group_rows(vwc_ref, lw), new_rows(3)])

    for j, branch in enumerate((o_cmp, o_sel, o_win)):
        for g in groups:
            for r in range(GQA_R):
                h = g * GQA_R + r
                for t in range(n_tok):
                    o_ref[j, 0, t:t + 1, h * HEAD_DIM:(h + 1) * HEAD_DIM] = branch[g][r * n_tok + t:r * n_tok + t + 1, :]


def _attn_sample(page_table, cache_ks, cache_vs, q_s, kcmp, vcmp, ks_new, vs_new, cache_kw, cache_vw, kw_new, vw_new,
                 expand):
    nb, n_pages = page_table.shape
    past = n_pages * PAGE_SIZE
    n_tok = ks_new.shape[0] // (nb * N_KV)
    rows = GQA_R * n_tok
    nblk = kcmp.shape[2]
    lw = cache_kw.shape[0] // (nb * N_KV)
    new = pl.BlockSpec((n_tok * N_KV, HEAD_DIM), lambda b, pt: (b, 0))
    cmp = pl.BlockSpec((1, N_KV, nblk, HEAD_DIM), lambda b, pt: (b, 0, 0, 0))
    win = pl.BlockSpec((lw * N_KV, HEAD_DIM), lambda b, pt: (b, 0))
    any_ = pl.BlockSpec(memory_space=pl.ANY)
    return pl.pallas_call(
        functools.partial(_attn_sample_kernel, n_pages=n_pages, n_tok=n_tok, nsel=min(N_SEL, past // CMP_BLK + 1)),
        grid_spec=pltpu.PrefetchScalarGridSpec(
            num_scalar_prefetch=1,
            grid=(nb,),
            in_specs=[any_, any_,
                      pl.BlockSpec((N_KV, 1, rows, HEAD_DIM), lambda b, pt: (0, b, 0, 0)),
                      cmp, cmp, new, new, win, win, new, new,
                      pl.BlockSpec(expand.shape, lambda b, pt: (0, 0))],
            out_specs=[pl.BlockSpec((3, 1, n_tok, D_ATT), lambda b, pt: (0, b, 0, 0)), win, win],
            scratch_shapes=[pltpu.VMEM((2, past * N_KV, HEAD_DIM), F32), pltpu.VMEM((2, past * N_KV, HEAD_DIM), F32),
                            pltpu.VMEM((4, PAGE_SIZE * N_KV, HEAD_DIM), F32),
                            pltpu.SemaphoreType.DMA((2,)), pltpu.SemaphoreType.DMA((2,))]),
        out_shape=[jax.ShapeDtypeStruct((3, nb, n_tok, D_ATT), F32),
                   jax.ShapeDtypeStruct(cache_kw.shape, F32), jax.ShapeDtypeStruct(cache_vw.shape, F32)],
        compiler_params=_cparams(("arbitrary",)),
        name="attn_sample",
    )(page_table, cache_ks, cache_vs, q_s, kcmp, vcmp, ks_new, vs_new, cache_kw, cache_vw, kw_new, vw_new, expand)


def _expand_matrix(rows, nkeys):
    return (jnp.arange(rows, dtype=jnp.int32)[:, None] == (jnp.arange(nkeys, dtype=jnp.int32) // CMP_BLK)[None, :]).astype(BF16)


def _pick(m, pref):
    while m % pref:
        pref //= 2
    return pref


def kernel(x_prompt, x_sample, cache_k_cmp, cache_v_cmp, cache_k_sel, cache_v_sel, cache_k_win, cache_v_win, state_h, state_conv, page_table, norm_mix_pre, w_in, conv_w, conv_b, w_ga, b_ga, w_gx, b_gx, lam, pe_k, pe_v, w_phi_k, w_phi_v, norm_rnn_out, norm_att_out, w_out, norm_mix_post, norm_mlp_pre, w_up, w_down, norm_mlp_post):
    depth = w_in.shape[0]
    assert depth == 1, "single layer only"
    bp, seq, d_model = x_prompt.shape
    bs, n_tok, _ = x_sample.shape
    d_rnn = conv_w.shape[-1]
    n_pages = page_table.shape[1]
    past = n_pages * PAGE_SIZE
    lw_in = cache_k_win.shape[2]
    assert lw_in == WINDOW and past % CMP_BLK == 0 and n_tok <= CMP_BLK and d_rnn == RNN_BLOCKS * RNN_BW
    l = 0

    w = w_in[l].astype(BF16)
    c0 = 2 * d_rnn
    q_blk0 = c0 // (GQA_R * HEAD_DIM)
    kv_blk0 = (c0 + D_ATT) // D_KV
    w_gl = jnp.pad(w[:, c0 + D_ATT + 6 * D_KV:], ((0, 0), (0, LANES - 3 * N_HEADS)))
    w_o = w_out[l].astype(BF16)
    w_u = w_up[l].astype(BF16)
    w_d = w_down[l].astype(BF16)
    wg = jnp.concatenate([w_ga[l], w_gx[l]], axis=-1).astype(BF16)
    wk_phi = w_phi_k[l].astype(BF16)
    wv_phi = w_phi_v[l].astype(BF16)
    vec = lambda a: a[l].reshape(1, -1)
    rnn_p = (conv_w[l], vec(conv_b), wg, vec(b_ga), vec(b_gx), vec(lam))

    def mixer_tail(x2d, o_rnn, o3, gates, tm):
        cat = _cat_norm(o_rnn, o3, gates, vec(norm_rnn_out), vec(norm_att_out), _pick(x2d.shape[0], 256))
        mix = _matmul(cat, w_o, tm, 512, name="out_proj")
        h, hn = _post_norm(mix, x2d, vec(norm_mix_post), vec(norm_mlp_pre), _pick(x2d.shape[0], 256), True)
        ff = _mlp(hn, w_u, w_d, _pick(x2d.shape[0], 512), 512)
        (y,) = _post_norm(ff, h, vec(norm_mlp_post), vec(norm_mlp_post), _pick(x2d.shape[0], 256), False)
        return y

    mp = bp * seq
    xp2 = x_prompt.reshape(mp, d_model)
    tm_p = _pick(seq, 1024)
    cos_p, sin_p = _rope_tables(jnp.arange(seq, dtype=jnp.int32))
    xn = _norm_cast(xp2, vec(norm_mix_pre), _pick(mp, 512))
    xy = _matmul(xn, w, tm_p, 512, name="proj_xy", n=c0)
    q_hm = _proj_q_prompt(xn, w, q_blk0, cos_p, sin_p, tm_p)
    *kv_nat, kv_hm = _proj_kv(xn, w, kv_blk0, cos_p, sin_p, _pick(seq, 512), True)
    gates = _matmul(xn, w_gl, tm_p, LANES, body=_mm_gate_kernel, name="proj_gate")
    o_rnn, h_p, cv_p = _rglru_prompt(xy, *rnn_p, bp, seq, _pick(seq, 128))
    nblk_p = seq // CMP_BLK
    cos_e, sin_e = _rope_tables(jnp.arange(nblk_p, dtype=jnp.int32) * CMP_BLK + CMP_BLK - 1)
    cmp_rows = _pick(seq, 1024)
    kcmp, vcmp = _cmp_prompt(kv_nat[0], kv_nat[1], pe_k[l], pe_v[l], wk_phi, wv_phi, cos_e, sin_e, bp, seq, cmp_rows)
    e_rows = max(LANES, nblk_p)
    o3 = _attn_prompt(q_hm, kcmp, vcmp, kv_hm, _expand_matrix(e_rows, seq), bp, seq, 128, _pick(seq, 512))
    y_p = mixer_tail(xp2, o_rnn, o3, gates, tm_p).reshape(bp, seq, d_model)
    lw_p = min(WINDOW, seq)
    nat5 = lambda a: a.reshape(1, bp, seq, N_KV, HEAD_DIM)
    p_out = (nat5(kv_nat[0]), nat5(kv_nat[1]), nat5(kv_nat[2]), nat5(kv_nat[3]),
             nat5(kv_nat[4])[:, :, seq - lw_p:], nat5(kv_nat[5])[:, :, seq - lw_p:],
             h_p.reshape(1, bp, d_rnn), cv_p.reshape(1, bp, CONV_W - 1, d_rnn))

    ms = bs * n_tok
    xs_tb = x_sample.swapaxes(0, 1).reshape(ms, d_model)
    pos_s = past + jnp.arange(n_tok, dtype=jnp.int32)
    cos_s, sin_s = _rope_tables(jnp.repeat(pos_s, bs))
    xn_s = _norm_cast(xs_tb, vec(norm_mix_pre), ms)
    xy_s = _matmul(xn_s, w, ms, 512, name="proj_xy", n=c0)
    q_s = _proj_q_sample(xn_s, w, q_blk0, cos_s, sin_s, n_tok)
    kv_s = _proj_kv(xn_s, w, kv_blk0, cos_s, sin_s, ms, False)
    gates_s = _matmul(xn_s, w_gl, ms, LANES, body=_mm_gate_kernel, name="proj_gate")
    o_rnn_s, h_s, cv_s = _rglru_sample(xy_s, state_conv[l].swapaxes(0, 1), state_h[l], *rnn_p, n_tok, past)
    kv_bt = [a.reshape(n_tok, bs, N_KV, HEAD_DIM).swapaxes(0, 1) for a in kv_s]
    kv_s2 = [a.reshape(ms * N_KV, HEAD_DIM) for a in kv_bt]
    rows2d = lambda c: c.reshape(-1, HEAD_DIM)
    nblk_c = past // CMP_BLK
    npad = 8
    cos_es, sin_es = _rope_tables(jnp.arange(nblk_c + npad, dtype=jnp.int32) * CMP_BLK + CMP_BLK - 1)
    kcmp_s, vcmp_s = _cmp_sample(page_table, rows2d(cache_k_cmp), rows2d(cache_v_cmp), kv_s2[0], kv_s2[1],
                                 pe_k[l], pe_v[l], wk_phi, wv_phi, cos_es, sin_es, npad)
    o3_s, kw_out, vw_out = _attn_sample(page_table, rows2d(cache_k_sel), rows2d(cache_v_sel), q_s, kcmp_s, vcmp_s,
                                        kv_s2[2], kv_s2[3], rows2d(cache_k_win), rows2d(cache_v_win), kv_s2[4], kv_s2[5],
                                        _expand_matrix(nblk_c + npad, past))
    gates_bt = gates_s.reshape(n_tok, bs, LANES).swapaxes(0, 1).reshape(ms, LANES)
    y_s = mixer_tail(x_sample.reshape(ms, d_model), o_rnn_s.reshape(ms, d_rnn), o3_s.reshape(3, ms, D_ATT), gates_bt,
                     ms).reshape(bs, n_tok, d_model)
    new5 = lambda a: a.reshape(1, bs, n_tok, N_KV, HEAD_DIM)
    s_out = (new5(kv_bt[0]), new5(kv_bt[1]), new5(kv_bt[2]), new5(kv_bt[3]),
             kw_out.reshape(cache_k_win.shape), vw_out.reshape(cache_v_win.shape),
             h_s.reshape(1, bs, d_rnn), cv_s.swapaxes(0, 1).reshape(1, bs, CONV_W - 1, d_rnn))
    return (y_p, y_s) + p_out + s_out
```

```python
import functools

import jax
import jax.numpy as jnp
import numpy as np
from jax import lax
from jax.experimental import pallas as pl
from jax.experimental.pallas import tpu as pltpu

F32 = jnp.float32
BF16 = jnp.bfloat16

HEAD_DIM = 128
N_KV = 4
GQA_R = 4
N_HEADS = N_KV * GQA_R
D_ATT = N_HEADS * HEAD_DIM
D_KV = N_KV * HEAD_DIM
CMP_BLK = 64
N_SEL = 16
WINDOW = 512
CONV_W = 4
C_GATE = 8.0
RNN_BLOCKS = 16
RNN_BW = 128
ROPE_THETA = 10000.0
EPS = 1e-6
NEG = -1e30
PAGE_SIZE = 128
SCALE = HEAD_DIM ** -0.5
LOG2E = 1.4426950408889634
QSCALE = SCALE * LOG2E
LANES = 128
VMEM_LIMIT = 56 * 1024 * 1024


def _cparams(sem):
    return pltpu.CompilerParams(dimension_semantics=sem, vmem_limit_bytes=VMEM_LIMIT)


def _rope_tables(pos):
    half = HEAD_DIM // 2
    inv = ROPE_THETA ** (-jnp.arange(half, dtype=F32) * (2.0 / HEAD_DIM))
    ang = pos.astype(F32)[:, None] * inv[None, :]
    c, s = jnp.cos(ang), jnp.sin(ang)
    return jnp.concatenate([c, c], axis=-1), jnp.concatenate([-s, s], axis=-1)


def _rope(x, cos2, sin2):
    return x * cos2 + pltpu.roll(x, HEAD_DIM // 2, 1) * sin2


def _rms(x):
    return x * lax.rsqrt(jnp.mean(x * x, axis=-1, keepdims=True) + EPS)


def _gelu(x):
    return 0.5 * x * (1.0 + jnp.tanh(0.7978845608028654 * (x + 0.044715 * (x * x * x))))


def _softplus(x):
    return jnp.maximum(x, 0.0) + jnp.log1p(jnp.exp(-jnp.abs(x)))


def _sigmoid(x):
    return 1.0 / (1.0 + jnp.exp(-x))


def _norm_kernel(x_ref, g_ref, o_ref):
    o_ref[...] = (_rms(x_ref[...]) * g_ref[...]).astype(o_ref.dtype)


def _norm_cast(x, g, tm):
    m, d = x.shape
    return pl.pallas_call(
        _norm_kernel,
        grid=(m // tm,),
        in_specs=[pl.BlockSpec((tm, d), lambda i: (i, 0)), pl.BlockSpec((1, d), lambda i: (0, 0))],
        out_specs=pl.BlockSpec((tm, d), lambda i: (i, 0)),
        out_shape=jax.ShapeDtypeStruct((m, d), BF16),
        compiler_params=_cparams(("parallel",)),
        name="norm_cast",
    )(x, g)


def _cat_norm_kernel(orn_ref, o3_ref, gate_ref, grn_ref, gat_ref, cat_ref, tmp):
    d = orn_ref.shape[1]
    cat_ref[:, :d] = (_rms(orn_ref[...].astype(F32)) * grn_ref[...]).astype(BF16)
    g = gate_ref[...]
    ss = jnp.zeros((g.shape[0], 1), F32)
    for h in range(N_HEADS):
        sl = slice(h * HEAD_DIM, (h + 1) * HEAD_DIM)
        oh = (g[:, 3 * h:3 * h + 1] * o3_ref[0, :, sl].astype(F32)
              + g[:, 3 * h + 1:3 * h + 2] * o3_ref[1, :, sl].astype(F32)
              + g[:, 3 * h + 2:3 * h + 3] * o3_ref[2, :, sl].astype(F32))
        tmp[:, sl] = oh
        ss = ss + jnp.sum(oh * oh, axis=-1, keepdims=True)
    inv = lax.rsqrt(ss * (1.0 / D_ATT) + EPS)
    cat_ref[:, d:] = (tmp[...] * inv * gat_ref[...]).astype(BF16)


def _cat_norm(o_rnn, o3, gates, g_rnn, g_att, tm):
    m, d = o_rnn.shape
    return pl.pallas_call(
        _cat_norm_kernel,
        grid=(m // tm,),
        in_specs=[pl.BlockSpec((tm, d), lambda i: (i, 0)),
                  pl.BlockSpec((3, tm, D_ATT), lambda i: (0, i, 0)),
                  pl.BlockSpec((tm, LANES), lambda i: (i, 0)),
                  pl.BlockSpec((1, d), lambda i: (0, 0)),
                  pl.BlockSpec((1, D_ATT), lambda i: (0, 0))],
        out_specs=pl.BlockSpec((tm, d + D_ATT), lambda i: (i, 0)),
        out_shape=jax.ShapeDtypeStruct((m, d + D_ATT), BF16),
        scratch_shapes=[pltpu.VMEM((tm, D_ATT), F32)],
        compiler_params=_cparams(("parallel",)),
        name="cat_norm",
    )(o_rnn, o3, gates, g_rnn, g_att)


def _dot(a, b):
    return jnp.dot(a, b, preferred_element_type=F32)


def _dot_nt(a, b):
    return lax.dot_general(a, b, (((1,), (1,)), ((), ())), preferred_element_type=F32)


def _mm_plain_kernel(a_ref, b_ref, o_ref):
    o_ref[...] = _dot(a_ref[...], b_ref[...])


def _mm_gate_kernel(a_ref, b_ref, o_ref):
    o_ref[...] = _sigmoid(_dot(a_ref[...], b_ref[...]))


def _matmul(a, b, tm, tn, body=_mm_plain_kernel, name="matmul", n=None, col0=0):
    m, k = a.shape
    n = b.shape[1] if n is None else n
    return pl.pallas_call(
        body,
        grid=(m // tm, n // tn),
        in_specs=[pl.BlockSpec((tm, k), lambda i, j: (i, 0)), pl.BlockSpec((k, tn), lambda i, j: (0, col0 + j))],
        out_specs=pl.BlockSpec((tm, tn), lambda i, j: (i, j)),
        out_shape=jax.ShapeDtypeStruct((m, n), F32),
        compiler_params=_cparams(("parallel", "arbitrary")),
        name=name,
    )(a, b)


def _mm_q_prompt_kernel(a_ref, b_ref, cos_ref, sin_ref, o_ref):
    acc = _dot(a_ref[...], b_ref[...])
    cos2, sin2 = cos_ref[...], sin_ref[...]
    for r in range(GQA_R):
        x = acc[:, r * HEAD_DIM:(r + 1) * HEAD_DIM]
        o_ref[r] = (_rope(x, cos2, sin2) * QSCALE).astype(BF16)


def _proj_q_prompt(a, w, col0, cos2, sin2, tm):
    m, k = a.shape
    nt = cos2.shape[0] // tm
    return pl.pallas_call(
        _mm_q_prompt_kernel,
        grid=(m // tm, N_KV),
        in_specs=[pl.BlockSpec((tm, k), lambda i, j: (i, 0)),
                  pl.BlockSpec((k, GQA_R * HEAD_DIM), lambda i, j: (0, col0 + j)),
                  pl.BlockSpec((tm, HEAD_DIM), lambda i, j: (i % nt, 0)),
                  pl.BlockSpec((tm, HEAD_DIM), lambda i, j: (i % nt, 0))],
        out_specs=pl.BlockSpec((GQA_R, tm, HEAD_DIM), lambda i, j: (j, i, 0)),
        out_shape=jax.ShapeDtypeStruct((N_HEADS, m, HEAD_DIM), BF16),
        compiler_params=_cparams(("parallel", "arbitrary")),
        name="proj_q_prompt",
    )(a, w, cos2, sin2)


def _mm_q_sample_kernel(a_ref, b_ref, cos_ref, sin_ref, o_ref, tmp, *, n_tok):
    acc = _dot(a_ref[...], b_ref[...])
    cos2, sin2 = cos_ref[...], sin_ref[...]
    nb = o_ref.shape[1]
    for r in range(GQA_R):
        x = acc[:, r * HEAD_DIM:(r + 1) * HEAD_DIM]
        tmp[...] = _rope(x, cos2, sin2) * QSCALE
        for t in range(n_tok):
            o_ref[0, :, n_tok * r + t, :] = tmp[t * nb:(t + 1) * nb, :]


def _proj_q_sample(a, w, col0, cos2, sin2, n_tok):
    m, k = a.shape
    nb = m // n_tok
    return pl.pallas_call(
        functools.partial(_mm_q_sample_kernel, n_tok=n_tok),
        grid=(N_KV,),
        in_specs=[pl.BlockSpec((m, k), lambda j: (0, 0)),
                  pl.BlockSpec((k, GQA_R * HEAD_DIM), lambda j: (0, col0 + j)),
                  pl.BlockSpec((m, HEAD_DIM), lambda j: (0, 0)),
                  pl.BlockSpec((m, HEAD_DIM), lambda j: (0, 0))],
        out_specs=pl.BlockSpec((1, nb, GQA_R * n_tok, HEAD_DIM), lambda j: (j, 0, 0, 0)),
        out_shape=jax.ShapeDtypeStruct((N_KV, nb, GQA_R * n_tok, HEAD_DIM), F32),
        scratch_shapes=[pltpu.VMEM((m, HEAD_DIM), F32)],
        compiler_params=_cparams(("arbitrary",)),
        name="proj_q_sample",
    )(a, w, cos2, sin2)


def _mm_kv_kernel(a_ref, b_ref, cos_ref, sin_ref, *outs, with_hm):
    j = pl.program_id(1)
    tm = a_ref.shape[0]
    acc = _dot(a_ref[...], b_ref[...])
    is_rope = jnp.logical_or(j == 2, j == 4)
    cos2 = jnp.where(is_rope, cos_ref[...], 1.0)
    sin2 = jnp.where(is_rope, sin_ref[...], 0.0)
    ys = [_rope(acc[:, g * HEAD_DIM:(g + 1) * HEAD_DIM], cos2, sin2) for g in range(N_KV)]
    for s in range(6):
        @pl.when(j == s)
        def _(s=s):
            for g in range(N_KV):
                outs[s][pl.ds(g, tm, stride=N_KV), :] = ys[g]
    if with_hm:
        for g in range(N_KV):
            outs[6][0, g] = ys[g].astype(BF16)


def _proj_kv(a, w, col0, cos2, sin2, tm, with_hm):
    m, k = a.shape
    nt = cos2.shape[0] // tm
    nat = pl.BlockSpec((tm * N_KV, HEAD_DIM), lambda i, j: (i, 0))
    out_specs = [nat] * 6
    out_shape = [jax.ShapeDtypeStruct((m * N_KV, HEAD_DIM), F32)] * 6
    if with_hm:
        out_specs = out_specs + [pl.BlockSpec((1, N_KV, tm, HEAD_DIM), lambda i, j: (j, 0, i, 0))]
        out_shape = out_shape + [jax.ShapeDtypeStruct((6, N_KV, m, HEAD_DIM), BF16)]
    return pl.pallas_call(
        functools.partial(_mm_kv_kernel, with_hm=with_hm),
        grid=(m // tm, 6),
        in_specs=[pl.BlockSpec((tm, k), lambda i, j: (i, 0)),
                  pl.BlockSpec((k, D_KV), lambda i, j: (0, col0 + j)),
                  pl.BlockSpec((tm, HEAD_DIM), lambda i, j: (i % nt, 0)),
                  pl.BlockSpec((tm, HEAD_DIM), lambda i, j: (i % nt, 0))],
        out_specs=out_specs,
        out_shape=out_shape,
        compiler_params=_cparams(("parallel", "arbitrary")),
        name="proj_kv",
    )(a, w, cos2, sin2)


ROW_CHUNK = 64


def _residual_copy(res_hbm, rbuf, sem):
    tm = rbuf.shape[0]
    return pltpu.make_async_copy(res_hbm.at[pl.ds(pl.multiple_of(pl.program_id(0) * tm, tm), tm)], rbuf, sem)


def _mlp_kernel(h_ref, wu_ref, wd_ref, res_hbm, g_ref, o_ref, rbuf, sem):
    j = pl.program_id(1)

    @pl.when(j == 0)
    def _():
        _residual_copy(res_hbm, rbuf, sem).start()
        o_ref[...] = jnp.zeros(o_ref.shape, F32)

    u = _dot(h_ref[...], wu_ref[...])
    f = jnp.square(jnp.maximum(u, 0.0)).astype(BF16)
    o_ref[...] += _dot(f, wd_ref[...])

    @pl.when(j == pl.num_programs(1) - 1)
    def _():
        _residual_copy(res_hbm, rbuf, sem).wait()
        for r in range(0, o_ref.shape[0], ROW_CHUNK):
            rows = slice(r, r + ROW_CHUNK)
            o_ref[rows, :] = rbuf[rows, :] + _rms(o_ref[rows, :]) * g_ref[...]


def _mlp(hn, w_up, w_down, res, g, tm, tf):
    m, d = hn.shape
    f = w_up.shape[1]
    return pl.pallas_call(
        _mlp_kernel,
        grid=(m // tm, f // tf),
        in_specs=[pl.BlockSpec((tm, d), lambda i, j: (i, 0)),
                  pl.BlockSpec((d, tf), lambda i, j: (0, j)),
                  pl.BlockSpec((tf, d), lambda i, j: (j, 0)),
                  pl.BlockSpec(memory_space=pl.ANY),
                  pl.BlockSpec((1, d), lambda i, j: (0, 0))],
        out_specs=pl.BlockSpec((tm, d), lambda i, j: (i, 0)),
        out_shape=jax.ShapeDtypeStruct((m, d), F32),
        scratch_shapes=[pltpu.VMEM((tm, d), F32), pltpu.SemaphoreType.DMA(())],
        compiler_params=_cparams(("arbitrary", "arbitrary")),
        name="mlp",
    )(hn, w_up, w_down, res, g)


def _out_proj_kernel(a_ref, b_ref, res_hbm, g_ref, g2_ref, h_ref, hn_ref, rbuf, sem):
    j = pl.program_id(1)
    tn = b_ref.shape[1]

    @pl.when(j == 0)
    def _():
        _residual_copy(res_hbm, rbuf, sem).start()

    h_ref[:, pl.ds(pl.multiple_of(j * tn, tn), tn)] = _dot(a_ref[...], b_ref[...])

    @pl.when(j == pl.num_programs(1) - 1)
    def _():
        _residual_copy(res_hbm, rbuf, sem).wait()
        for r in range(0, h_ref.shape[0], ROW_CHUNK):
            rows = slice(r, r + ROW_CHUNK)
            y = rbuf[rows, :] + _rms(h_ref[rows, :]) * g_ref[...]
            h_ref[rows, :] = y
            hn_ref[rows, :] = (_rms(y) * g2_ref[...]).astype(BF16)


def _out_proj(a, b, res, g, g2, tm, tn):
    m, k = a.shape
    n = b.shape[1]
    row = pl.BlockSpec((tm, n), lambda i, j: (i, 0))
    vec = pl.BlockSpec((1, n), lambda i, j: (0, 0))
    return pl.pallas_call(
        _out_proj_kernel,
        grid=(m // tm, n // tn),
        in_specs=[pl.BlockSpec((tm, k), lambda i, j: (i, 0)), pl.BlockSpec((k, tn), lambda i, j: (0, j)),
                  pl.BlockSpec(memory_space=pl.ANY), vec, vec],
        out_specs=[row, row],
        out_shape=[jax.ShapeDtypeStruct((m, n), F32), jax.ShapeDtypeStruct((m, n), BF16)],
        scratch_shapes=[pltpu.VMEM((tm, n), F32), pltpu.SemaphoreType.DMA(())],
        compiler_params=_cparams(("arbitrary", "arbitrary")),
        name="out_proj",
    )(a, b, res, g, g2)


def _rglru_gates(xc, gg, bga, bgx, lam, first_row):
    ra = _sigmoid(gg[:, :RNN_BW] + bga)
    rx = _sigmoid(gg[:, RNN_BW:] + bgx)
    log_a = (-C_GATE) * ra * _softplus(-lam)
    a = jnp.exp(log_a)
    th = jnp.tanh(log_a)
    mult = jnp.sqrt((-2.0) * th / (1.0 - th))
    if first_row is not None:
        mult = jnp.where(first_row, 1.0, mult)
    return a, mult * rx * xc


def _rglru_prompt_kernel(xr_ref, yr_ref, cw_ref, cb_ref, wg_ref, bga_ref, bgx_ref, lam_ref,
                         o_ref, h_ref, conv_ref, xbuf, hcar):
    ti = pl.program_id(1)
    tt = xr_ref.shape[0]

    @pl.when(ti == 0)
    def _():
        xbuf[0:8, :] = jnp.zeros((8, xbuf.shape[1]), F32)
        hcar[...] = jnp.zeros(hcar.shape, F32)

    xbuf[8:8 + tt, :] = xr_ref[...]
    row = lax.broadcasted_iota(jnp.int32, (tt, RNN_BW), 0)
    first_row = jnp.logical_and(row == 0, ti == 0)

    def block(n, carry):
        col = pl.ds(pl.multiple_of(n * RNN_BW, RNN_BW), RNN_BW)
        xc = cb_ref[:, col]
        for k in range(CONV_W):
            xc = xc + xbuf[8 - (CONV_W - 1) + k:8 - (CONV_W - 1) + k + tt, col] * cw_ref[k:k + 1, col]
        gg = _dot(xc.astype(BF16), wg_ref[n])
        a, b = _rglru_gates(xc, gg, bga_ref[:, col], bgx_ref[:, col], lam_ref[:, col], first_row)
        s = 1
        while s < tt:
            keep = row >= s
            a_sh = jnp.where(keep, pltpu.roll(a, s, 0), 1.0)
            b_sh = jnp.where(keep, pltpu.roll(b, s, 0), 0.0)
            b = a * b_sh + b
            a = a * a_sh
            s *= 2
        hs = a * hcar[:, col] + b
        hcar[:, col] = hs[tt - 1:tt, :]
        o_ref[:, col] = (hs * _gelu(yr_ref[:, col])).astype(o_ref.dtype)
        return carry

    lax.fori_loop(0, RNN_BLOCKS, block, 0, unroll=2)
    xbuf[0:8, :] = xbuf[tt:tt + 8, :]

    @pl.when(ti == pl.num_programs(1) - 1)
    def _():
        h_ref[0] = hcar[...]
        conv_ref[0] = xbuf[8 - (CONV_W - 1):8, :]


def _rglru_prompt(xy, conv_w, conv_b, wg, b_ga, b_gx, lam, nbatch, seq, tt):
    d = conv_w.shape[1]
    nt = seq // tt
    vec = pl.BlockSpec((1, d), lambda b, t: (0, 0))
    return pl.pallas_call(
        _rglru_prompt_kernel,
        grid=(nbatch, nt),
        in_specs=[pl.BlockSpec((tt, d), lambda b, t: (b * nt + t, 0)),
                  pl.BlockSpec((tt, d), lambda b, t: (b * nt + t, 1)),
                  pl.BlockSpec((CONV_W, d), lambda b, t: (0, 0)),
                  vec,
                  pl.BlockSpec((RNN_BLOCKS, RNN_BW, 2 * RNN_BW), lambda b, t: (0, 0, 0)),
                  vec, vec, vec],
        out_specs=[pl.BlockSpec((tt, d), lambda b, t: (b * nt + t, 0)),
                   pl.BlockSpec((1, 1, d), lambda b, t: (b, 0, 0)),
                   pl.BlockSpec((1, CONV_W - 1, d), lambda b, t: (b, 0, 0))],
        out_shape=[jax.ShapeDtypeStruct((nbatch * seq, d), BF16),
                   jax.ShapeDtypeStruct((nbatch, 1, d), F32),
                   jax.ShapeDtypeStruct((nbatch, CONV_W - 1, d), F32)],
        scratch_shapes=[pltpu.VMEM((tt + 8, d), F32), pltpu.VMEM((1, d), F32)],
        compiler_params=_cparams(("arbitrary", "arbitrary")),
        name="rglru_prompt",
    )(xy, xy, conv_w, conv_b, wg, b_ga, b_gx, lam)


def _rglru_sample_kernel(xr_ref, yr_ref, cst_ref, h0_ref, cw_ref, cb_ref, wg_ref, bga_ref, bgx_ref, lam_ref,
                         o_ref, h_ref, conv_ref, *, n_tok, pos0):
    nb = h0_ref.shape[0]
    xs = [cst_ref[k] for k in range(CONV_W - 1)] + [xr_ref[t * nb:(t + 1) * nb, :] for t in range(n_tok)]
    h = h0_ref[...]
    for t in range(n_tok):
        xc = cb_ref[...]
        for k in range(CONV_W):
            xc = xc + xs[t + k] * cw_ref[k:k + 1, :]
        gg = _dot(xc.astype(BF16), wg_ref[0])
        a, b = _rglru_gates(xc, gg, bga_ref[...], bgx_ref[...], lam_ref[...], True if pos0 + t == 0 else None)
        h = a * h + b
        o_ref[:, t, :] = h * _gelu(yr_ref[t * nb:(t + 1) * nb, :])
    h_ref[...] = h
    for k in range(CONV_W - 1):
        conv_ref[k] = xs[n_tok + k]


def _rglru_sample(xy, cst, h0, conv_w, conv_b, wg, b_ga, b_gx, lam, n_tok, pos0):
    nb, d = h0.shape
    m = nb * n_tok
    vec = pl.BlockSpec((1, RNN_BW), lambda n: (0, n))
    st = pl.BlockSpec((CONV_W - 1, nb, RNN_BW), lambda n: (0, 0, n))
    return pl.pallas_call(
        functools.partial(_rglru_sample_kernel, n_tok=n_tok, pos0=pos0),
        grid=(RNN_BLOCKS,),
        in_specs=[pl.BlockSpec((m, RNN_BW), lambda n: (0, n)),
                  pl.BlockSpec((m, RNN_BW), lambda n: (0, RNN_BLOCKS + n)),
                  st,
                  pl.BlockSpec((nb, RNN_BW), lambda n: (0, n)),
                  pl.BlockSpec((CONV_W, RNN_BW), lambda n: (0, n)),
                  vec,
                  pl.BlockSpec((1, RNN_BW, 2 * RNN_BW), lambda n: (n, 0, 0)),
                  vec, vec, vec],
        out_specs=[pl.BlockSpec((nb, n_tok, RNN_BW), lambda n: (0, 0, n)),
                   pl.BlockSpec((nb, RNN_BW), lambda n: (0, n)),
                   st],
        out_shape=[jax.ShapeDtypeStruct((nb, n_tok, d), F32),
                   jax.ShapeDtypeStruct((nb, d), F32),
                   jax.ShapeDtypeStruct((CONV_W - 1, nb, d), F32)],
        compiler_params=_cparams(("parallel",)),
        name="rglru_sample",
    )(xy, xy, cst, h0, conv_w, conv_b, wg, b_ga, b_gx, lam)


def _block_summary(x, pe_ref, w_ref, nblk):
    m = jnp.sum(x.reshape(nblk, CMP_BLK, HEAD_DIM), axis=1) * (1.0 / CMP_BLK)
    m = m + jnp.mean(pe_ref[...], axis=0, keepdims=True)
    return _dot(m.astype(BF16), w_ref[...])


def _cmp_prompt_kernel(kc_ref, vc_ref, pek_ref, pev_ref, wk_ref, wv_ref, cos_ref, sin_ref, ko_ref, vo_ref):
    nblk = ko_ref.shape[2]
    rows = nblk * CMP_BLK
    for g in range(N_KV):
        kk = _block_summary(kc_ref[pl.ds(g, rows, stride=N_KV), :], pek_ref, wk_ref, nblk)
        ko_ref[0, g] = _rope(kk, cos_ref[...], sin_ref[...]).astype(BF16)
        vo_ref[0, g] = _block_summary(vc_ref[pl.ds(g, rows, stride=N_KV), :], pev_ref, wv_ref, nblk).astype(BF16)


def _cmp_prompt(kc, vc, pe_k, pe_v, w_k, w_v, cos_e, sin_e, nbatch, seq, rows):
    nblk = rows // CMP_BLK
    nt = seq // rows
    nat = pl.BlockSpec((rows * N_KV, HEAD_DIM), lambda b, t: (b * nt + t, 0))
    full = lambda shape: pl.BlockSpec(shape, lambda b, t: (0,) * len(shape))
    tab = pl.BlockSpec((nblk, HEAD_DIM), lambda b, t: (t, 0))
    out = pl.BlockSpec((1, N_KV, nblk, HEAD_DIM), lambda b, t: (b, 0, t, 0))
    shape = jax.ShapeDtypeStruct((nbatch, N_KV, seq // CMP_BLK, HEAD_DIM), BF16)
    return pl.pallas_call(
        _cmp_prompt_kernel,
        grid=(nbatch, nt),
        in_specs=[nat, nat, full((CMP_BLK, HEAD_DIM)), full((CMP_BLK, HEAD_DIM)),
                  full((HEAD_DIM, HEAD_DIM)), full((HEAD_DIM, HEAD_DIM)), tab, tab],
        out_specs=[out, out],
        out_shape=[shape, shape],
        compiler_params=_cparams(("parallel", "arbitrary")),
        name="cmp_prompt",
    )(kc, vc, pe_k, pe_v, w_k, w_v, cos_e, sin_e)


def _page_copies(pt_ref, hbm, buf, sem, bb, slot, n_pages):
    pr = PAGE_SIZE * N_KV
    return [pltpu.make_async_copy(hbm.at[pl.ds(pl.multiple_of(pt_ref[bb, p] * pr, pr), pr)],
                                  buf.at[slot, pl.ds(p * pr, pr)], sem.at[slot])
            for p in range(n_pages)]


def _gather_step(pt_ref, hbms, bufs, sems, n_pages):
    b = pl.program_id(0)
    slot = b % 2

    @pl.when(b == 0)
    def _():
        for hbm, buf, sem in zip(hbms, bufs, sems):
            for c in _page_copies(pt_ref, hbm, buf, sem, 0, 0, n_pages):
                c.start()

    @pl.when(b + 1 < pl.num_programs(0))
    def _():
        for hbm, buf, sem in zip(hbms, bufs, sems):
            for c in _page_copies(pt_ref, hbm, buf, sem, b + 1, 1 - slot, n_pages):
                c.start()

    for hbm, buf, sem in zip(hbms, bufs, sems):
        for c in _page_copies(pt_ref, hbm, buf, sem, b, slot, n_pages):
            c.wait()
    return slot


def _cmp_sample_kernel(pt_ref, kc_hbm, vc_hbm, kn_ref, vn_ref, pek_ref, pev_ref, wk_ref, wv_ref, cos_ref, sin_ref,
                       ko_ref, vo_ref, kbuf, vbuf, ksem, vsem, *, n_pages):
    slot = _gather_step(pt_ref, (kc_hbm, vc_hbm), (kbuf, vbuf), (ksem, vsem), n_pages)
    past = n_pages * PAGE_SIZE
    nblk = past // CMP_BLK
    npad = ko_ref.shape[2] - nblk
    n_tok = kn_ref.shape[0] // N_KV

    def tail(new_ref, pe_ref, w_ref, g):
        m = (jnp.sum(new_ref[pl.ds(g, n_tok, stride=N_KV), :], axis=0, keepdims=True)
             + jnp.sum(pe_ref[...], axis=0, keepdims=True)) * (1.0 / CMP_BLK)
        return _dot(jnp.broadcast_to(m, (npad, HEAD_DIM)).astype(BF16), w_ref[...])

    for g in range(N_KV):
        kk = _block_summary(kbuf[slot, pl.ds(g, past, stride=N_KV), :], pek_ref, wk_ref, nblk)
        ko_ref[0, g, 0:nblk, :] = _rope(kk, cos_ref[0:nblk, :], sin_ref[0:nblk, :])
        kt = tail(kn_ref, pek_ref, wk_ref, g)
        ko_ref[0, g, nblk:nblk + npad, :] = _rope(kt, cos_ref[nblk:nblk + npad, :], sin_ref[nblk:nblk + npad, :])
        vo_ref[0, g, 0:nblk, :] = _block_summary(vbuf[slot, pl.ds(g, past, stride=N_KV), :], pev_ref, wv_ref, nblk)
        vo_ref[0, g, nblk:nblk + npad, :] = tail(vn_ref, pev_ref, wv_ref, g)


def _cmp_sample(page_table, cache_k, cache_v, k_new, v_new, pe_k, pe_v, w_k, w_v, cos_e, sin_e, npad):
    nb, n_pages = page_table.shape
    past = n_pages * PAGE_SIZE
    nblk = past // CMP_BLK
    n_tok = k_new.shape[0] // (nb * N_KV)
    full = lambda shape: pl.BlockSpec(shape, lambda b, pt: (0,) * len(shape))
    new = pl.BlockSpec((n_tok * N_KV, HEAD_DIM), lambda b, pt: (b, 0))
    out = pl.BlockSpec((1, N_KV, nblk + npad, HEAD_DIM), lambda b, pt: (b, 0, 0, 0))
    shape = jax.ShapeDtypeStruct((nb, N_KV, nblk + npad, HEAD_DIM), F32)
    return pl.pallas_call(
        functools.partial(_cmp_sample_kernel, n_pages=n_pages),
        grid_spec=pltpu.PrefetchScalarGridSpec(
            num_scalar_prefetch=1,
            grid=(nb,),
            in_specs=[pl.BlockSpec(memory_space=pl.ANY), pl.BlockSpec(memory_space=pl.ANY), new, new,
                      full((CMP_BLK, HEAD_DIM)), full((CMP_BLK, HEAD_DIM)),
                      full((HEAD_DIM, HEAD_DIM)), full((HEAD_DIM, HEAD_DIM)),
                      full((nblk + npad, HEAD_DIM)), full((nblk + npad, HEAD_DIM))],
            out_specs=[out, out],
            scratch_shapes=[pltpu.VMEM((2, past * N_KV, HEAD_DIM), F32), pltpu.VMEM((2, past * N_KV, HEAD_DIM), F32),
                            pltpu.SemaphoreType.DMA((2,)), pltpu.SemaphoreType.DMA((2,))]),
        out_shape=[shape, shape],
        compiler_params=_cparams(("arbitrary",)),
        name="cmp_sample",
    )(page_table, cache_k, cache_v, k_new, v_new, pe_k, pe_v, w_k, w_v, cos_e, sin_e)


def _select_blocks_t(imp_t, cur, nsel):
    nblk = imp_t.shape[0]
    nidx = lax.broadcasted_iota(jnp.int32, imp_t.shape, 0)
    forced = jnp.logical_or(jnp.logical_or(nidx == cur, nidx == cur - 1), nidx == 0)
    imp_t = jnp.where(forced, 1e4, imp_t)
    imp_t = jnp.where(nidx > cur, -1.0, imp_t)
    rank = jnp.zeros(imp_t.shape, F32)
    for i in range(nblk):
        row = imp_t[i:i + 1, :]
        rank = rank + jnp.where(nidx > i, jnp.where(row >= imp_t, 1.0, 0.0), jnp.where(row > imp_t, 1.0, 0.0))
    return jnp.where(rank < float(nsel), 1.0, 0.0)


def _softmax_lanes(s):
    m = jnp.max(s, axis=-1, keepdims=True)
    e = jnp.exp2(s - m)
    return e / jnp.sum(e, axis=-1, keepdims=True)


def _attn_prompt_kernel(q_ref, kc_ref, vc_ref, ks_ref, vs_ref, kw_ref, vw_ref, e_ref, o_ref, *, kt, kv_unroll, nblk, nsel):
    qi = pl.program_id(2)
    tq = q_ref.shape[1]
    nblk_pad = kc_ref.shape[2]
    hh = GQA_R
    nch = GQA_R // hh
    rows = hh * tq
    qs = [q_ref[c * hh:(c + 1) * hh].reshape(rows, HEAD_DIM) for c in range(nch)]
    qpos = qi * tq + lax.broadcasted_iota(jnp.int32, (tq, 1), 0)
    any_valid = jnp.where(qpos >= CMP_BLK - 1, 1.0, 0.0)

    kc = kc_ref[0, 0]
    blk_end = lax.broadcasted_iota(jnp.int32, (1, nblk_pad), 1) * CMP_BLK + (CMP_BLK - 1)
    valid = jnp.logical_and(blk_end <= qpos, blk_end < nblk * CMP_BLK)
    o_cmp = []
    imp = jnp.zeros((tq, nblk_pad), F32)
    for c in range(nch):
        s = jnp.where(valid[None], _dot_nt(qs[c], kc).reshape(hh, tq, nblk_pad), NEG)
        e = jnp.exp2(s - jnp.max(s, axis=-1, keepdims=True))
        inv = any_valid[None] / jnp.sum(e, axis=-1, keepdims=True)
        o_cmp.append(_dot(e.reshape(rows, nblk_pad).astype(BF16), vc_ref[0, 0]) * inv.reshape(rows, 1))
        imp = imp + jnp.sum(e * inv, axis=0)

    qpos_l = qi * tq + lax.broadcasted_iota(jnp.int32, (1, tq), 1)
    sel_t = _select_blocks_t(imp.T[:nblk], qpos_l // CMP_BLK, nsel)
    pad = e_ref.shape[0] - nblk
    if pad:
        sel_t = jnp.concatenate([sel_t, jnp.zeros((pad, tq), F32)], axis=0)
    sel = sel_t.T.astype(BF16)

    def write_heads(branch, outs):
        for c in range(nch):
            for r in range(hh):
                cs = slice((c * hh + r) * HEAD_DIM, (c * hh + r + 1) * HEAD_DIM)
                o_ref[branch, :, cs] = outs[c][r * tq:(r + 1) * tq].astype(BF16)

    span = WINDOW + tq
    start = pl.multiple_of(jnp.maximum(qi * tq - WINDOW, 0), tq)
    kpos_w = start + lax.broadcasted_iota(jnp.int32, (1, span), 1)
    band = jnp.logical_and(kpos_w <= qpos, kpos_w > qpos - WINDOW)
    k_w = kw_ref[0, 0, pl.ds(start, span), :]
    v_w = vw_ref[0, 0, pl.ds(start, span), :]
    o_win = []
    for c in range(nch):
        sw = jnp.where(band[None], _dot_nt(qs[c], k_w).reshape(hh, tq, span), NEG)
        ew = jnp.exp2(sw - jnp.max(sw, axis=-1, keepdims=True))
        inv_w = 1.0 / jnp.sum(ew, axis=-1, keepdims=True)
        o_win.append(_dot(ew.reshape(rows, span).astype(BF16), v_w) * inv_w.reshape(rows, 1))
    write_heads(0, o_cmp)
    write_heads(2, o_win)

    def kv_tile(j, carry):
        off = pl.multiple_of(j * kt, kt)
        kpos = off + lax.broadcasted_iota(jnp.int32, (1, kt), 1)
        hit = _dot(sel, e_ref[:, pl.ds(off, kt)])
        bias = jnp.where(kpos <= qpos, (hit - 1.0) * (-NEG), NEG)
        k_t = ks_ref[0, 0, pl.ds(off, kt), :]
        v_t = vs_ref[0, 0, pl.ds(off, kt), :]
        out = []
        for c in range(nch):
            m_i, l_i, acc = carry[3 * c:3 * c + 3]
            sj = _dot_nt(qs[c], k_t).reshape(hh, tq, kt) + bias[None]
            m_new = jnp.maximum(m_i, jnp.max(sj, axis=-1, keepdims=True))
            alpha = jnp.exp2(m_i - m_new)
            pj = jnp.exp2(sj - m_new)
            l_new = alpha * l_i + jnp.sum(pj, axis=-1, keepdims=True)
            pv = _dot(pj.reshape(rows, kt).astype(BF16), v_t)
            out += [m_new, l_new, acc * alpha.reshape(rows, 1) + pv]
        return tuple(out)

    def kv_step(jj, carry):
        for u in range(kv_unroll):
            carry = kv_tile(jj * kv_unroll + u, carry)
        return carry

    n_tiles = (qi * tq + tq + kt - 1) // kt
    init = (jnp.full((hh, tq, 1), NEG, F32), jnp.zeros((hh, tq, 1), F32), jnp.zeros((rows, HEAD_DIM), F32)) * nch
    fin = lax.fori_loop(0, n_tiles // kv_unroll, kv_step, init)
    for u in range(kv_unroll - 1):
        j_left = (n_tiles // kv_unroll) * kv_unroll + u
        fin = lax.cond(j_left < n_tiles, functools.partial(kv_tile, j_left), lambda c: c, fin)
    o_sel = [fin[3 * c + 2] / fin[3 * c + 1].reshape(rows, 1) for c in range(nch)]

    write_heads(1, o_sel)


def _attn_prompt(q_hm, kcmp, vcmp, kv_hm, expand, nbatch, seq, tq, kt):
    nq = seq // tq
    nblk = seq // CMP_BLK
    nblk_pad = expand.shape[0]
    kcmp, vcmp = (jnp.pad(a, ((0, 0), (0, 0), (0, nblk_pad - nblk), (0, 0))) for a in (kcmp, vcmp))
    kv = lambda seg: pl.BlockSpec((1, 1, seq, HEAD_DIM), lambda b, g, i: (seg, g, b, 0))
    cmp = pl.BlockSpec((1, 1, nblk_pad, HEAD_DIM), lambda b, g, i: (b, g, 0, 0))
    return pl.pallas_call(
        functools.partial(_attn_prompt_kernel, kt=kt, kv_unroll=2, nblk=nblk, nsel=min(N_SEL, nblk)),
        grid=(nbatch, N_KV, nq),
        in_specs=[pl.BlockSpec((GQA_R, tq, HEAD_DIM), lambda b, g, i: (g, b * nq + i, 0)),
                  cmp, cmp, kv(2), kv(3), kv(4), kv(5),
                  pl.BlockSpec(expand.shape, lambda b, g, i: (0, 0))],
        out_specs=pl.BlockSpec((3, tq, GQA_R * HEAD_DIM), lambda b, g, i: (0, b * nq + i, g)),
        out_shape=jax.ShapeDtypeStruct((3, nbatch * seq, D_ATT), BF16),
        compiler_params=_cparams(("parallel", "parallel", "arbitrary")),
        name="attn_prompt",
    )(q_hm, kcmp, vcmp, kv_hm, kv_hm, kv_hm, kv_hm, expand)


def _attn_sample_kernel(pt_ref, ks_hbm, vs_hbm, q_ref, kc_ref, vc_ref, ksn_ref, vsn_ref, kwc_ref, vwc_ref,
                        kwn_ref, vwn_ref, e_ref, o_ref, okw_ref, ovw_ref, kbuf, vbuf, newbuf, ksem, vsem,
                        *, n_pages, n_tok, nsel):
    b = pl.program_id(0)
    slot = _gather_step(pt_ref, (ks_hbm, vs_hbm), (kbuf, vbuf), (ksem, vsem), n_pages)
    past = n_pages * PAGE_SIZE
    nblk_c = past // CMP_BLK
    nblk = kc_ref.shape[2]
    rows = GQA_R * n_tok
    lw = kwc_ref.shape[0] // N_KV
    nr = n_tok * N_KV

    @pl.when(b == 0)
    def _():
        newbuf[...] = jnp.zeros(newbuf.shape, F32)

    for i, ref in enumerate((ksn_ref, vsn_ref, kwn_ref, vwn_ref)):
        newbuf[i, 0:nr, :] = ref[...]

    okw_ref[0:(lw - n_tok) * N_KV, :] = kwc_ref[nr:lw * N_KV, :]
    okw_ref[(lw - n_tok) * N_KV:lw * N_KV, :] = kwn_ref[...]
    ovw_ref[0:(lw - n_tok) * N_KV, :] = vwc_ref[nr:lw * N_KV, :]
    ovw_ref[(lw - n_tok) * N_KV:lw * N_KV, :] = vwn_ref[...]

    rows_all = N_KV * rows
    t_row = lax.broadcasted_iota(jnp.int32, (rows_all, 1), 0) % n_tok
    qpos = past + t_row
    nidx = lax.broadcasted_iota(jnp.int32, (rows_all, nblk), 1)
    valid = nidx * CMP_BLK + (CMP_BLK - 1) <= qpos
    cur = qpos // CMP_BLK
    forced = jnp.logical_or(jnp.logical_or(nidx == cur, nidx == cur - 1), nidx == 0)
    new_lane = lax.broadcasted_iota(jnp.int32, (1, PAGE_SIZE), 1)
    new_ok = jnp.logical_and(new_lane < n_tok, past + new_lane <= qpos)
    wpos = (past - lw) + lax.broadcasted_iota(jnp.int32, (1, lw), 1)
    win_ok = jnp.logical_and(wpos <= qpos, wpos > qpos - WINDOW)
    win_new_ok = jnp.logical_and(new_ok, past + new_lane > qpos - WINDOW)

    groups = range(N_KV)
    grp = lambda x, g: x[g * rows:(g + 1) * rows]
    stack = lambda f: jnp.concatenate([f(g) for g in groups], axis=0)
    group_rows = lambda ref, n: (lambda g: ref[pl.ds(g, n, stride=N_KV), :].astype(BF16))
    new_rows = lambda i: (lambda g: newbuf[i, pl.ds(g, PAGE_SIZE, stride=N_KV), :].astype(BF16))
    qs = [q_ref[g, 0].astype(BF16) for g in groups]

    def attend(s_parts, v_parts):
        ps = _softmax_lanes(jnp.concatenate(s_parts, axis=1)).astype(BF16)
        outs = []
        for g in groups:
            pg, off, acc = grp(ps, g), 0, None
            for sp, vp in zip(s_parts, v_parts):
                d = _dot(pg[:, off:off + sp.shape[1]], vp(g))
                acc = d if acc is None else acc + d
                off += sp.shape[1]
            outs.append(acc)
        return outs

    p = _softmax_lanes(jnp.where(valid, stack(lambda g: _dot_nt(qs[g], kc_ref[0, g].astype(BF16))), NEG))
    p = p * jnp.where(qpos >= CMP_BLK - 1, 1.0, 0.0)
    o_cmp = [_dot(grp(p, g).astype(BF16), vc_ref[0, g].astype(BF16)) for g in groups]

    def head_sum(pg):
        u = pg + pltpu.roll(pg, (GQA_R // 2) * n_tok, 0)
        return u + pltpu.roll(u, n_tok, 0)

    imp = stack(lambda g: head_sum(grp(p, g)))
    imp = jnp.where(forced, 1e4, imp)
    imp = jnp.where(nidx > cur, -1.0, imp)
    rank = jnp.zeros((rows_all, nblk), F32)
    for i in range(nblk):
        col = imp[:, i:i + 1]
        rank = rank + jnp.where(nidx > i, jnp.where(col >= imp, 1.0, 0.0), jnp.where(col > imp, 1.0, 0.0))
    sel = jnp.where(rank < float(nsel), 1.0, 0.0)

    hit = _dot(sel.astype(BF16), e_ref[...])
    k_cached = lambda g: kbuf[slot, pl.ds(g, past, stride=N_KV), :].astype(BF16)
    v_cached = lambda g: vbuf[slot, pl.ds(g, past, stride=N_KV), :].astype(BF16)
    s_c = jnp.where(hit > 0.5, stack(lambda g: _dot_nt(qs[g], k_cached(g))), NEG)
    new_sel = jnp.logical_and(new_ok, sel[:, nblk_c:nblk_c + 1] > 0.5)
    s_n = jnp.where(new_sel, stack(lambda g: _dot_nt(qs[g], new_rows(0)(g))), NEG)
    o_sel = attend([s_c, s_n], [v_cached, new_rows(1)])

    s_w = jnp.where(win_ok, stack(lambda g: _dot_nt(qs[g], group_rows(kwc_ref, lw)(g))), NEG)
    s_wn = jnp.where(win_new_ok, stack(lambda g: _dot_nt(qs[g], new_rows(2)(g))), NEG)
    o_win = attend([s_w, s_wn], [group_rows(vwc_ref, lw), new_rows(3)])

    for j, branch in enumerate((o_cmp, o_sel, o_win)):
        for g in groups:
            for r in range(GQA_R):
                h = g * GQA_R + r
                for t in range(n_tok):
                    o_ref[j, 0, t:t + 1, h * HEAD_DIM:(h + 1) * HEAD_DIM] = branch[g][r * n_tok + t:r * n_tok + t + 1, :]


def _attn_sample(page_table, cache_ks, cache_vs, q_s, kcmp, vcmp, ks_new, vs_new, cache_kw, cache_vw, kw_new, vw_new,
                 expand):
    nb, n_pages = page_table.shape
    past = n_pages * PAGE_SIZE
    n_tok = ks_new.shape[0] // (nb * N_KV)
    rows = GQA_R * n_tok
    nblk = kcmp.shape[2]
    lw = cache_kw.shape[0] // (nb * N_KV)
    new = pl.BlockSpec((n_tok * N_KV, HEAD_DIM), lambda b, pt: (b, 0))
    cmp = pl.BlockSpec((1, N_KV, nblk, HEAD_DIM), lambda b, pt: (b, 0, 0, 0))
    win = pl.BlockSpec((lw * N_KV, HEAD_DIM), lambda b, pt: (b, 0))
    any_ = pl.BlockSpec(memory_space=pl.ANY)
    return pl.pallas_call(
        functools.partial(_attn_sample_kernel, n_pages=n_pages, n_tok=n_tok, nsel=min(N_SEL, past // CMP_BLK + 1)),
        grid_spec=pltpu.PrefetchScalarGridSpec(
            num_scalar_prefetch=1,
            grid=(nb,),
            in_specs=[any_, any_,
                      pl.BlockSpec((N_KV, 1, rows, HEAD_DIM), lambda b, pt: (0, b, 0, 0)),
                      cmp, cmp, new, new, win, win, new, new,
                      pl.BlockSpec(expand.shape, lambda b, pt: (0, 0))],
            out_specs=[pl.BlockSpec((3, 1, n_tok, D_ATT), lambda b, pt: (0, b, 0, 0)), win, win],
            scratch_shapes=[pltpu.VMEM((2, past * N_KV, HEAD_DIM), F32), pltpu.VMEM((2, past * N_KV, HEAD_DIM), F32),
                            pltpu.VMEM((4, PAGE_SIZE * N_KV, HEAD_DIM), F32),
                            pltpu.SemaphoreType.DMA((2,)), pltpu.SemaphoreType.DMA((2,))]),
        out_shape=[jax.ShapeDtypeStruct((3, nb, n_tok, D_ATT), F32),
                   jax.ShapeDtypeStruct(cache_kw.shape, F32), jax.ShapeDtypeStruct(cache_vw.shape, F32)],
        compiler_params=_cparams(("arbitrary",)),
        name="attn_sample",
    )(page_table, cache_ks, cache_vs, q_s, kcmp, vcmp, ks_new, vs_new, cache_kw, cache_vw, kw_new, vw_new, expand)


def _expand_matrix(rows, nkeys):
    return (jnp.arange(rows, dtype=jnp.int32)[:, None] == (jnp.arange(nkeys, dtype=jnp.int32) // CMP_BLK)[None, :]).astype(BF16)


def _pick(m, pref):
    while m % pref:
        pref //= 2
    return pref


def kernel(x_prompt, x_sample, cache_k_cmp, cache_v_cmp, cache_k_sel, cache_v_sel, cache_k_win, cache_v_win, state_h, state_conv, page_table, norm_mix_pre, w_in, conv_w, conv_b, w_ga, b_ga, w_gx, b_gx, lam, pe_k, pe_v, w_phi_k, w_phi_v, norm_rnn_out, norm_att_out, w_out, norm_mix_post, norm_mlp_pre, w_up, w_down, norm_mlp_post):
    depth = w_in.shape[0]
    assert depth == 1, "single layer only"
    bp, seq, d_model = x_prompt.shape
    bs, n_tok, _ = x_sample.shape
    d_rnn = conv_w.shape[-1]
    n_pages = page_table.shape[1]
    past = n_pages * PAGE_SIZE
    lw_in = cache_k_win.shape[2]
    assert lw_in == WINDOW and past % CMP_BLK == 0 and n_tok <= CMP_BLK and d_rnn == RNN_BLOCKS * RNN_BW
    l = 0

    w = w_in[l].astype(BF16)
    c0 = 2 * d_rnn
    q_blk0 = c0 // (GQA_R * HEAD_DIM)
    kv_blk0 = (c0 + D_ATT) // D_KV
    w_gl = jnp.pad(w[:, c0 + D_ATT + 6 * D_KV:], ((0, 0), (0, LANES - 3 * N_HEADS)))
    w_o = w_out[l].astype(BF16)
    w_u = w_up[l].astype(BF16)
    w_d = w_down[l].astype(BF16)
    wg = jnp.concatenate([w_ga[l], w_gx[l]], axis=-1).astype(BF16)
    wk_phi = w_phi_k[l].astype(BF16)
    wv_phi = w_phi_v[l].astype(BF16)
    vec = lambda a: a[l].reshape(1, -1)
    rnn_p = (conv_w[l], vec(conv_b), wg, vec(b_ga), vec(b_gx), vec(lam))

    def mixer_tail(x2d, o_rnn, o3, gates):
        cat = _cat_norm(o_rnn, o3, gates, vec(norm_rnn_out), vec(norm_att_out), _pick(x2d.shape[0], 256))
        tm_t = _pick(x2d.shape[0], 512)
        h, hn = _out_proj(cat, w_o, x2d, vec(norm_mix_post), vec(norm_mlp_pre), tm_t, 512)
        return _mlp(hn, w_u, w_d, h, vec(norm_mlp_post), tm_t, 512)

    mp = bp * seq
    xp2 = x_prompt.reshape(mp, d_model)
    tm_p = _pick(seq, 1024)
    cos_p, sin_p = _rope_tables(jnp.arange(seq, dtype=jnp.int32))
    xn = _norm_cast(xp2, vec(norm_mix_pre), _pick(mp, 512))
    xy = _matmul(xn, w, tm_p, 512, name="proj_xy", n=c0)
    q_hm = _proj_q_prompt(xn, w, q_blk0, cos_p, sin_p, tm_p)
    *kv_nat, kv_hm = _proj_kv(xn, w, kv_blk0, cos_p, sin_p, _pick(seq, 512), True)
    gates = _matmul(xn, w_gl, tm_p, LANES, body=_mm_gate_kernel, name="proj_gate")
    o_rnn, h_p, cv_p = _rglru_prompt(xy, *rnn_p, bp, seq, _pick(seq, 128))
    nblk_p = seq // CMP_BLK
    cos_e, sin_e = _rope_tables(jnp.arange(nblk_p, dtype=jnp.int32) * CMP_BLK + CMP_BLK - 1)
    cmp_rows = _pick(seq, 1024)
    kcmp, vcmp = _cmp_prompt(kv_nat[0], kv_nat[1], pe_k[l], pe_v[l], wk_phi, wv_phi, cos_e, sin_e, bp, seq, cmp_rows)
    e_rows = max(LANES, nblk_p)
    o3 = _attn_prompt(q_hm, kcmp, vcmp, kv_hm, _expand_matrix(e_rows, seq), bp, seq, 128, _pick(seq, 512))
    y_p = mixer_tail(xp2, o_rnn, o3, gates).reshape(bp, seq, d_model)
    lw_p = min(WINDOW, seq)
    nat5 = lambda a: a.reshape(1, bp, seq, N_KV, HEAD_DIM)
    p_out = (nat5(kv_nat[0]), nat5(kv_nat[1]), nat5(kv_nat[2]), nat5(kv_nat[3]),
             nat5(kv_nat[4])[:, :, seq - lw_p:], nat5(kv_nat[5])[:, :, seq - lw_p:],
             h_p.reshape(1, bp, d_rnn), cv_p.reshape(1, bp, CONV_W - 1, d_rnn))

    ms = bs * n_tok
    xs_tb = x_sample.swapaxes(0, 1).reshape(ms, d_model)
    pos_s = past + jnp.arange(n_tok, dtype=jnp.int32)
    cos_s, sin_s = _rope_tables(jnp.repeat(pos_s, bs))
    xn_s = _norm_cast(xs_tb, vec(norm_mix_pre), ms)
    xy_s = _matmul(xn_s, w, ms, 512, name="proj_xy", n=c0)
    q_s = _proj_q_sample(xn_s, w, q_blk0, cos_s, sin_s, n_tok)
    kv_s = _proj_kv(xn_s, w, kv_blk0, cos_s, sin_s, ms, False)
    gates_s = _matmul(xn_s, w_gl, ms, LANES, body=_mm_gate_kernel, name="proj_gate")
    o_rnn_s, h_s, cv_s = _rglru_sample(xy_s, state_conv[l].swapaxes(0, 1), state_h[l], *rnn_p, n_tok, past)
    kv_bt = [a.reshape(n_tok, bs, N_KV, HEAD_DIM).swapaxes(0, 1) for a in kv_s]
    kv_s2 = [a.reshape(ms * N_KV, HEAD_DIM) for a in kv_bt]
    rows2d = lambda c: c.reshape(-1, HEAD_DIM)
    nblk_c = past // CMP_BLK
    npad = 8
    cos_es, sin_es = _rope_tables(jnp.arange(nblk_c + npad, dtype=jnp.int32) * CMP_BLK + CMP_BLK - 1)
    kcmp_s, vcmp_s = _cmp_sample(page_table, rows2d(cache_k_cmp), rows2d(cache_v_cmp), kv_s2[0], kv_s2[1],
                                 pe_k[l], pe_v[l], wk_phi, wv_phi, cos_es, sin_es, npad)
    o3_s, kw_out, vw_out = _attn_sample(page_table, rows2d(cache_k_sel), rows2d(cache_v_sel), q_s, kcmp_s, vcmp_s,
                                        kv_s2[2], kv_s2[3], rows2d(cache_k_win), rows2d(cache_v_win), kv_s2[4], kv_s2[5],
                                        _expand_matrix(nblk_c + npad, past))
    gates_bt = gates_s.reshape(n_tok, bs, LANES).swapaxes(0, 1).reshape(ms, LANES)
    y_s = mixer_tail(x_sample.reshape(ms, d_model), o_rnn_s.reshape(ms, d_rnn), o3_s.reshape(3, ms, D_ATT),
                     gates_bt).reshape(bs, n_tok, d_model)
    new5 = lambda a: a.reshape(1, bs, n_tok, N_KV, HEAD_DIM)
    s_out = (new5(kv_bt[0]), new5(kv_bt[1]), new5(kv_bt[2]), new5(kv_bt[3]),
             kw_out.reshape(cache_k_win.shape), vw_out.reshape(cache_v_win.shape),
             h_s.reshape(1, bs, d_rnn), cv_s.swapaxes(0, 1).reshape(1, bs, CONV_W - 1, d_rnn))
    return (y_p, y_s) + p_out + s_out
```

```python
import functools

import jax
import jax.numpy as jnp
import numpy as np
from jax import lax
from jax.experimental import pallas as pl
from jax.experimental.pallas import tpu as pltpu

F32 = jnp.float32
BF16 = jnp.bfloat16

HEAD_DIM = 128
N_KV = 4
GQA_R = 4
N_HEADS = N_KV * GQA_R
D_ATT = N_HEADS * HEAD_DIM
D_KV = N_KV * HEAD_DIM
CMP_BLK = 64
N_SEL = 16
WINDOW = 512
CONV_W = 4
C_GATE = 8.0
RNN_BLOCKS = 16
RNN_BW = 128
ROPE_THETA = 10000.0
EPS = 1e-6
NEG = -1e30
PAGE_SIZE = 128
SCALE = HEAD_DIM ** -0.5
LOG2E = 1.4426950408889634
QSCALE = SCALE * LOG2E
LANES = 128
VMEM_LIMIT = 56 * 1024 * 1024


def _cparams(sem):
    return pltpu.CompilerParams(dimension_semantics=sem, vmem_limit_bytes=VMEM_LIMIT)


def _rope_tables(pos):
    half = HEAD_DIM // 2
    inv = ROPE_THETA ** (-jnp.arange(half, dtype=F32) * (2.0 / HEAD_DIM))
    ang = pos.astype(F32)[:, None] * inv[None, :]
    c, s = jnp.cos(ang), jnp.sin(ang)
    return jnp.concatenate([c, c], axis=-1), jnp.concatenate([-s, s], axis=-1)


def _rope(x, cos2, sin2):
    return x * cos2 + pltpu.roll(x, HEAD_DIM // 2, 1) * sin2


def _rms(x):
    return x * lax.rsqrt(jnp.mean(x * x, axis=-1, keepdims=True) + EPS)


def _gelu(x):
    return 0.5 * x * (1.0 + jnp.tanh(0.7978845608028654 * (x + 0.044715 * (x * x * x))))


def _softplus(x):
    return jnp.maximum(x, 0.0) + jnp.log1p(jnp.exp(-jnp.abs(x)))


def _sigmoid(x):
    return 1.0 / (1.0 + jnp.exp(-x))


def _norm_kernel(x_ref, g_ref, o_ref):
    o_ref[...] = (_rms(x_ref[...]) * g_ref[...]).astype(o_ref.dtype)


def _norm_cast(x, g, tm):
    m, d = x.shape
    return pl.pallas_call(
        _norm_kernel,
        grid=(m // tm,),
        in_specs=[pl.BlockSpec((tm, d), lambda i: (i, 0)), pl.BlockSpec((1, d), lambda i: (0, 0))],
        out_specs=pl.BlockSpec((tm, d), lambda i: (i, 0)),
        out_shape=jax.ShapeDtypeStruct((m, d), BF16),
        compiler_params=_cparams(("parallel",)),
        name="norm_cast",
    )(x, g)


def _cat_norm_kernel(orn_ref, o3_ref, gate_ref, grn_ref, gat_ref, cat_ref, tmp):
    d = orn_ref.shape[1]
    cat_ref[:, :d] = (_rms(orn_ref[...].astype(F32)) * grn_ref[...]).astype(BF16)
    g = gate_ref[...]
    ss = jnp.zeros((g.shape[0], 1), F32)
    for h in range(N_HEADS):
        sl = slice(h * HEAD_DIM, (h + 1) * HEAD_DIM)
        oh = (g[:, 3 * h:3 * h + 1] * o3_ref[0, :, sl].astype(F32)
              + g[:, 3 * h + 1:3 * h + 2] * o3_ref[1, :, sl].astype(F32)
              + g[:, 3 * h + 2:3 * h + 3] * o3_ref[2, :, sl].astype(F32))
        tmp[:, sl] = oh
        ss = ss + jnp.sum(oh * oh, axis=-1, keepdims=True)
    inv = lax.rsqrt(ss * (1.0 / D_ATT) + EPS)
    cat_ref[:, d:] = (tmp[...] * inv * gat_ref[...]).astype(BF16)


def _cat_norm(o_rnn, o3, gates, g_rnn, g_att, tm):
    m, d = o_rnn.shape
    return pl.pallas_call(
        _cat_norm_kernel,
        grid=(m // tm,),
        in_specs=[pl.BlockSpec((tm, d), lambda i: (i, 0)),
                  pl.BlockSpec((3, tm, D_ATT), lambda i: (0, i, 0)),
                  pl.BlockSpec((tm, LANES), lambda i: (i, 0)),
                  pl.BlockSpec((1, d), lambda i: (0, 0)),
                  pl.BlockSpec((1, D_ATT), lambda i: (0, 0))],
        out_specs=pl.BlockSpec((tm, d + D_ATT), lambda i: (i, 0)),
        out_shape=jax.ShapeDtypeStruct((m, d + D_ATT), BF16),
        scratch_shapes=[pltpu.VMEM((tm, D_ATT), F32)],
        compiler_params=_cparams(("parallel",)),
        name="cat_norm",
    )(o_rnn, o3, gates, g_rnn, g_att)


def _dot(a, b):
    return jnp.dot(a, b, preferred_element_type=F32)


def _dot_nt(a, b):
    return lax.dot_general(a, b, (((1,), (1,)), ((), ())), preferred_element_type=F32)


def _mm_plain_kernel(a_ref, b_ref, o_ref):
    o_ref[...] = _dot(a_ref[...], b_ref[...])


def _mm_gate_kernel(a_ref, b_ref, o_ref):
    o_ref[...] = _sigmoid(_dot(a_ref[...], b_ref[...]))


def _matmul(a, b, tm, tn, body=_mm_plain_kernel, name="matmul", n=None, col0=0):
    m, k = a.shape
    n = b.shape[1] if n is None else n
    return pl.pallas_call(
        body,
        grid=(m // tm, n // tn),
        in_specs=[pl.BlockSpec((tm, k), lambda i, j: (i, 0)), pl.BlockSpec((k, tn), lambda i, j: (0, col0 + j))],
        out_specs=pl.BlockSpec((tm, tn), lambda i, j: (i, j)),
        out_shape=jax.ShapeDtypeStruct((m, n), F32),
        compiler_params=_cparams(("parallel", "arbitrary")),
        name=name,
    )(a, b)


def _mm_q_prompt_kernel(a_ref, b_ref, cos_ref, sin_ref, o_ref):
    acc = _dot(a_ref[...], b_ref[...])
    cos2, sin2 = cos_ref[...], sin_ref[...]
    for r in range(GQA_R):
        x = acc[:, r * HEAD_DIM:(r + 1) * HEAD_DIM]
        o_ref[r] = (_rope(x, cos2, sin2) * QSCALE).astype(BF16)


def _proj_q_prompt(a, w, col0, cos2, sin2, tm):
    m, k = a.shape
    nt = cos2.shape[0] // tm
    return pl.pallas_call(
        _mm_q_prompt_kernel,
        grid=(m // tm, N_KV),
        in_specs=[pl.BlockSpec((tm, k), lambda i, j: (i, 0)),
                  pl.BlockSpec((k, GQA_R * HEAD_DIM), lambda i, j: (0, col0 + j)),
                  pl.BlockSpec((tm, HEAD_DIM), lambda i, j: (i % nt, 0)),
                  pl.BlockSpec((tm, HEAD_DIM), lambda i, j: (i % nt, 0))],
        out_specs=pl.BlockSpec((GQA_R, tm, HEAD_DIM), lambda i, j: (j, i, 0)),
        out_shape=jax.ShapeDtypeStruct((N_HEADS, m, HEAD_DIM), BF16),
        compiler_params=_cparams(("parallel", "arbitrary")),
        name="proj_q_prompt",
    )(a, w, cos2, sin2)


def _mm_q_sample_kernel(a_ref, b_ref, cos_ref, sin_ref, o_ref, tmp, *, n_tok):
    acc = _dot(a_ref[...], b_ref[...])
    cos2, sin2 = cos_ref[...], sin_ref[...]
    nb = o_ref.shape[1]
    for r in range(GQA_R):
        x = acc[:, r * HEAD_DIM:(r + 1) * HEAD_DIM]
        tmp[...] = _rope(x, cos2, sin2) * QSCALE
        for t in range(n_tok):
            o_ref[0, :, n_tok * r + t, :] = tmp[t * nb:(t + 1) * nb, :]


def _proj_q_sample(a, w, col0, cos2, sin2, n_tok):
    m, k = a.shape
    nb = m // n_tok
    return pl.pallas_call(
        functools.partial(_mm_q_sample_kernel, n_tok=n_tok),
        grid=(N_KV,),
        in_specs=[pl.BlockSpec((m, k), lambda j: (0, 0)),
                  pl.BlockSpec((k, GQA_R * HEAD_DIM), lambda j: (0, col0 + j)),
                  pl.BlockSpec((m, HEAD_DIM), lambda j: (0, 0)),
                  pl.BlockSpec((m, HEAD_DIM), lambda j: (0, 0))],
        out_specs=pl.BlockSpec((1, nb, GQA_R * n_tok, HEAD_DIM), lambda j: (j, 0, 0, 0)),
        out_shape=jax.ShapeDtypeStruct((N_KV, nb, GQA_R * n_tok, HEAD_DIM), F32),
        scratch_shapes=[pltpu.VMEM((m, HEAD_DIM), F32)],
        compiler_params=_cparams(("arbitrary",)),
        name="proj_q_sample",
    )(a, w, cos2, sin2)


def _mm_kv_kernel(a_ref, b_ref, *refs, rope, with_hm):
    tm = a_ref.shape[0]
    acc = _dot(a_ref[...], b_ref[...])
    outs = refs[2:] if rope else refs
    for g in range(N_KV):
        y = acc[:, g * HEAD_DIM:(g + 1) * HEAD_DIM]
        if rope:
            y = _rope(y, refs[0][...], refs[1][...])
        outs[0][pl.ds(g, tm, stride=N_KV), :] = y
        if with_hm:
            outs[1][g] = y.astype(BF16)


def _proj_kv(a, w, col, cos2, sin2, tm, rope, with_hm):
    m, k = a.shape
    nt = cos2.shape[0] // tm
    tab = pl.BlockSpec((tm, HEAD_DIM), lambda i: (i % nt, 0))
    out_specs = [pl.BlockSpec((tm * N_KV, HEAD_DIM), lambda i: (i, 0))]
    out_shape = [jax.ShapeDtypeStruct((m * N_KV, HEAD_DIM), F32)]
    if with_hm:
        out_specs.append(pl.BlockSpec((N_KV, tm, HEAD_DIM), lambda i: (0, i, 0)))
        out_shape.append(jax.ShapeDtypeStruct((N_KV, m, HEAD_DIM), BF16))
    return pl.pallas_call(
        functools.partial(_mm_kv_kernel, rope=rope, with_hm=with_hm),
        grid=(m // tm,),
        in_specs=[pl.BlockSpec((tm, k), lambda i: (i, 0)), pl.BlockSpec((k, D_KV), lambda i: (0, col))]
                 + ([tab, tab] if rope else []),
        out_specs=out_specs,
        out_shape=out_shape,
        compiler_params=_cparams(("parallel",)),
        name="proj_kv",
    )(a, w, *((cos2, sin2) if rope else ()))


KV_ROPE = (False, False, True, False, True, False)


ROW_CHUNK = 64


def _residual_copy(res_hbm, rbuf, sem):
    tm = rbuf.shape[0]
    return pltpu.make_async_copy(res_hbm.at[pl.ds(pl.multiple_of(pl.program_id(0) * tm, tm), tm)], rbuf, sem)


def _mlp_kernel(h_ref, wu_ref, wd_ref, res_hbm, g_ref, o_ref, rbuf, sem):
    j = pl.program_id(1)

    @pl.when(j == 0)
    def _():
        _residual_copy(res_hbm, rbuf, sem).start()
        o_ref[...] = jnp.zeros(o_ref.shape, F32)

    u = _dot(h_ref[...], wu_ref[...])
    f = jnp.square(jnp.maximum(u, 0.0)).astype(BF16)
    o_ref[...] += _dot(f, wd_ref[...])

    @pl.when(j == pl.num_programs(1) - 1)
    def _():
        _residual_copy(res_hbm, rbuf, sem).wait()
        for r in range(0, o_ref.shape[0], ROW_CHUNK):
            rows = slice(r, r + ROW_CHUNK)
            o_ref[rows, :] = rbuf[rows, :] + _rms(o_ref[rows, :]) * g_ref[...]


def _mlp(hn, w_up, w_down, res, g, tm, tf):
    m, d = hn.shape
    f = w_up.shape[1]
    return pl.pallas_call(
        _mlp_kernel,
        grid=(m // tm, f // tf),
        in_specs=[pl.BlockSpec((tm, d), lambda i, j: (i, 0)),
                  pl.BlockSpec((d, tf), lambda i, j: (0, j)),
                  pl.BlockSpec((tf, d), lambda i, j: (j, 0)),
                  pl.BlockSpec(memory_space=pl.ANY),
                  pl.BlockSpec((1, d), lambda i, j: (0, 0))],
        out_specs=pl.BlockSpec((tm, d), lambda i, j: (i, 0)),
        out_shape=jax.ShapeDtypeStruct((m, d), F32),
        scratch_shapes=[pltpu.VMEM((tm, d), F32), pltpu.SemaphoreType.DMA(())],
        compiler_params=_cparams(("arbitrary", "arbitrary")),
        name="mlp",
    )(hn, w_up, w_down, res, g)


def _out_proj_kernel(a_ref, b_ref, res_hbm, g_ref, g2_ref, h_ref, hn_ref, rbuf, sem):
    j = pl.program_id(1)
    tn = b_ref.shape[1]

    @pl.when(j == 0)
    def _():
        _residual_copy(res_hbm, rbuf, sem).start()

    h_ref[:, pl.ds(pl.multiple_of(j * tn, tn), tn)] = _dot(a_ref[...], b_ref[...])

    @pl.when(j == pl.num_programs(1) - 1)
    def _():
        _residual_copy(res_hbm, rbuf, sem).wait()
        for r in range(0, h_ref.shape[0], ROW_CHUNK):
            rows = slice(r, r + ROW_CHUNK)
            y = rbuf[rows, :] + _rms(h_ref[rows, :]) * g_ref[...]
            h_ref[rows, :] = y
            hn_ref[rows, :] = (_rms(y) * g2_ref[...]).astype(BF16)


def _out_proj(a, b, res, g, g2, tm, tn):
    m, k = a.shape
    n = b.shape[1]
    row = pl.BlockSpec((tm, n), lambda i, j: (i, 0))
    vec = pl.BlockSpec((1, n), lambda i, j: (0, 0))
    b_mode = {"pipeline_mode": pl.Buffered(1)} if tn == n else {}
    return pl.pallas_call(
        _out_proj_kernel,
        grid=(m // tm, n // tn),
        in_specs=[pl.BlockSpec((tm, k), lambda i, j: (i, 0)), pl.BlockSpec((k, tn), lambda i, j: (0, j), **b_mode),
                  pl.BlockSpec(memory_space=pl.ANY), vec, vec],
        out_specs=[row, row],
        out_shape=[jax.ShapeDtypeStruct((m, n), F32), jax.ShapeDtypeStruct((m, n), BF16)],
        scratch_shapes=[pltpu.VMEM((tm, n), F32), pltpu.SemaphoreType.DMA(())],
        compiler_params=_cparams(("arbitrary", "arbitrary")),
        name="out_proj",
    )(a, b, res, g, g2)


def _rglru_gates(xc, gg, bga, bgx, lam, first_row):
    ra = _sigmoid(gg[:, :RNN_BW] + bga)
    rx = _sigmoid(gg[:, RNN_BW:] + bgx)
    log_a = (-C_GATE) * ra * _softplus(-lam)
    a = jnp.exp(log_a)
    th = jnp.tanh(log_a)
    mult = jnp.sqrt((-2.0) * th / (1.0 - th))
    if first_row is not None:
        mult = jnp.where(first_row, 1.0, mult)
    return a, mult * rx * xc


def _rglru_prompt_kernel(xr_ref, yr_ref, cw_ref, cb_ref, wg_ref, bga_ref, bgx_ref, lam_ref,
                         o_ref, h_ref, conv_ref, xbuf, hcar):
    ti = pl.program_id(1)
    tt = xr_ref.shape[0]

    @pl.when(ti == 0)
    def _():
        xbuf[0:8, :] = jnp.zeros((8, xbuf.shape[1]), F32)
        hcar[...] = jnp.zeros(hcar.shape, F32)

    xbuf[8:8 + tt, :] = xr_ref[...]
    row = lax.broadcasted_iota(jnp.int32, (tt, RNN_BW), 0)
    first_row = jnp.logical_and(row == 0, ti == 0)

    def block(n, carry):
        col = pl.ds(pl.multiple_of(n * RNN_BW, RNN_BW), RNN_BW)
        xc = cb_ref[:, col]
        for k in range(CONV_W):
            xc = xc + xbuf[8 - (CONV_W - 1) + k:8 - (CONV_W - 1) + k + tt, col] * cw_ref[k:k + 1, col]
        gg = _dot(xc.astype(BF16), wg_ref[n])
        a, b = _rglru_gates(xc, gg, bga_ref[:, col], bgx_ref[:, col], lam_ref[:, col], first_row)
        s = 1
        while s < tt:
            keep = row >= s
            a_sh = jnp.where(keep, pltpu.roll(a, s, 0), 1.0)
            b_sh = jnp.where(keep, pltpu.roll(b, s, 0), 0.0)
            b = a * b_sh + b
            a = a * a_sh
            s *= 2
        hs = a * hcar[:, col] + b
        hcar[:, col] = hs[tt - 1:tt, :]
        o_ref[:, col] = (hs * _gelu(yr_ref[:, col])).astype(o_ref.dtype)
        return carry

    lax.fori_loop(0, RNN_BLOCKS, block, 0, unroll=2)
    xbuf[0:8, :] = xbuf[tt:tt + 8, :]

    @pl.when(ti == pl.num_programs(1) - 1)
    def _():
        h_ref[0] = hcar[...]
        conv_ref[0] = xbuf[8 - (CONV_W - 1):8, :]


def _rglru_prompt(xy, conv_w, conv_b, wg, b_ga, b_gx, lam, nbatch, seq, tt):
    d = conv_w.shape[1]
    nt = seq // tt
    vec = pl.BlockSpec((1, d), lambda b, t: (0, 0))
    return pl.pallas_call(
        _rglru_prompt_kernel,
        grid=(nbatch, nt),
        in_specs=[pl.BlockSpec((tt, d), lambda b, t: (b * nt + t, 0)),
                  pl.BlockSpec((tt, d), lambda b, t: (b * nt + t, 1)),
                  pl.BlockSpec((CONV_W, d), lambda b, t: (0, 0)),
                  vec,
                  pl.BlockSpec((RNN_BLOCKS, RNN_BW, 2 * RNN_BW), lambda b, t: (0, 0, 0)),
                  vec, vec, vec],
        out_specs=[pl.BlockSpec((tt, d), lambda b, t: (b * nt + t, 0)),
                   pl.BlockSpec((1, 1, d), lambda b, t: (b, 0, 0)),
                   pl.BlockSpec((1, CONV_W - 1, d), lambda b, t: (b, 0, 0))],
        out_shape=[jax.ShapeDtypeStruct((nbatch * seq, d), BF16),
                   jax.ShapeDtypeStruct((nbatch, 1, d), F32),
                   jax.ShapeDtypeStruct((nbatch, CONV_W - 1, d), F32)],
        scratch_shapes=[pltpu.VMEM((tt + 8, d), F32), pltpu.VMEM((1, d), F32)],
        compiler_params=_cparams(("arbitrary", "arbitrary")),
        name="rglru_prompt",
    )(xy, xy, conv_w, conv_b, wg, b_ga, b_gx, lam)


def _rglru_sample_kernel(xr_ref, yr_ref, cst_ref, h0_ref, cw_ref, cb_ref, wg_ref, bga_ref, bgx_ref, lam_ref,
                         o_ref, h_ref, conv_ref, *, n_tok, pos0):
    nb = h0_ref.shape[0]
    xs = [cst_ref[k] for k in range(CONV_W - 1)] + [xr_ref[t * nb:(t + 1) * nb, :] for t in range(n_tok)]
    h = h0_ref[...]
    for t in range(n_tok):
        xc = cb_ref[...]
        for k in range(CONV_W):
            xc = xc + xs[t + k] * cw_ref[k:k + 1, :]
        gg = _dot(xc.astype(BF16), wg_ref[0])
        a, b = _rglru_gates(xc, gg, bga_ref[...], bgx_ref[...], lam_ref[...], True if pos0 + t == 0 else None)
        h = a * h + b
        o_ref[:, t, :] = h * _gelu(yr_ref[t * nb:(t + 1) * nb, :])
    h_ref[...] = h
    for k in range(CONV_W - 1):
        conv_ref[k] = xs[n_tok + k]


def _rglru_sample(xy, cst, h0, conv_w, conv_b, wg, b_ga, b_gx, lam, n_tok, pos0):
    nb, d = h0.shape
    m = nb * n_tok
    vec = pl.BlockSpec((1, RNN_BW), lambda n: (0, n))
    st = pl.BlockSpec((CONV_W - 1, nb, RNN_BW), lambda n: (0, 0, n))
    return pl.pallas_call(
        functools.partial(_rglru_sample_kernel, n_tok=n_tok, pos0=pos0),
        grid=(RNN_BLOCKS,),
        in_specs=[pl.BlockSpec((m, RNN_BW), lambda n: (0, n)),
                  pl.BlockSpec((m, RNN_BW), lambda n: (0, RNN_BLOCKS + n)),
                  st,
                  pl.BlockSpec((nb, RNN_BW), lambda n: (0, n)),
                  pl.BlockSpec((CONV_W, RNN_BW), lambda n: (0, n)),
                  vec,
                  pl.BlockSpec((1, RNN_BW, 2 * RNN_BW), lambda n: (n, 0, 0)),
                  vec, vec, vec],
        out_specs=[pl.BlockSpec((nb, n_tok, RNN_BW), lambda n: (0, 0, n)),
                   pl.BlockSpec((nb, RNN_BW), lambda n: (0, n)),
                   st],
        out_shape=[jax.ShapeDtypeStruct((nb, n_tok, d), F32),
                   jax.ShapeDtypeStruct((nb, d), F32),
                   jax.ShapeDtypeStruct((CONV_W - 1, nb, d), F32)],
        compiler_params=_cparams(("parallel",)),
        name="rglru_sample",
    )(xy, xy, cst, h0, conv_w, conv_b, wg, b_ga, b_gx, lam)


def _block_summary(x, pe_ref, w_ref, nblk):
    m = jnp.sum(x.reshape(nblk, CMP_BLK, HEAD_DIM), axis=1) * (1.0 / CMP_BLK)
    m = m + jnp.mean(pe_ref[...], axis=0, keepdims=True)
    return _dot(m.astype(BF16), w_ref[...])


def _cmp_prompt_kernel(kc_ref, vc_ref, pek_ref, pev_ref, wk_ref, wv_ref, cos_ref, sin_ref, ko_ref, vo_ref):
    nblk = ko_ref.shape[2]
    rows = nblk * CMP_BLK
    for g in range(N_KV):
        kk = _block_summary(kc_ref[pl.ds(g, rows, stride=N_KV), :], pek_ref, wk_ref, nblk)
        ko_ref[0, g] = _rope(kk, cos_ref[...], sin_ref[...]).astype(BF16)
        vo_ref[0, g] = _block_summary(vc_ref[pl.ds(g, rows, stride=N_KV), :], pev_ref, wv_ref, nblk).astype(BF16)


def _cmp_prompt(kc, vc, pe_k, pe_v, w_k, w_v, cos_e, sin_e, nbatch, seq, rows):
    nblk = rows // CMP_BLK
    nt = seq // rows
    nat = pl.BlockSpec((rows * N_KV, HEAD_DIM), lambda b, t: (b * nt + t, 0))
    full = lambda shape: pl.BlockSpec(shape, lambda b, t: (0,) * len(shape))
    tab = pl.BlockSpec((nblk, HEAD_DIM), lambda b, t: (t, 0))
    out = pl.BlockSpec((1, N_KV, nblk, HEAD_DIM), lambda b, t: (b, 0, t, 0))
    shape = jax.ShapeDtypeStruct((nbatch, N_KV, seq // CMP_BLK, HEAD_DIM), BF16)
    return pl.pallas_call(
        _cmp_prompt_kernel,
        grid=(nbatch, nt),
        in_specs=[nat, nat, full((CMP_BLK, HEAD_DIM)), full((CMP_BLK, HEAD_DIM)),
                  full((HEAD_DIM, HEAD_DIM)), full((HEAD_DIM, HEAD_DIM)), tab, tab],
        out_specs=[out, out],
        out_shape=[shape, shape],
        compiler_params=_cparams(("parallel", "arbitrary")),
        name="cmp_prompt",
    )(kc, vc, pe_k, pe_v, w_k, w_v, cos_e, sin_e)


def _page_copies(pt_ref, hbm, buf, sem, bb, slot, n_pages):
    pr = PAGE_SIZE * N_KV
    return [pltpu.make_async_copy(hbm.at[pl.ds(pl.multiple_of(pt_ref[bb, p] * pr, pr), pr)],
                                  buf.at[slot, pl.ds(p * pr, pr)], sem.at[slot])
            for p in range(n_pages)]


def _gather_step(pt_ref, hbms, bufs, sems, n_pages):
    b = pl.program_id(0)
    slot = b % 2

    @pl.when(b == 0)
    def _():
        for hbm, buf, sem in zip(hbms, bufs, sems):
            for c in _page_copies(pt_ref, hbm, buf, sem, 0, 0, n_pages):
                c.start()

    @pl.when(b + 1 < pl.num_programs(0))
    def _():
        for hbm, buf, sem in zip(hbms, bufs, sems):
            for c in _page_copies(pt_ref, hbm, buf, sem, b + 1, 1 - slot, n_pages):
                c.start()

    for hbm, buf, sem in zip(hbms, bufs, sems):
        for c in _page_copies(pt_ref, hbm, buf, sem, b, slot, n_pages):
            c.wait()
    return slot


def _cmp_sample_kernel(pt_ref, kc_hbm, vc_hbm, kn_ref, vn_ref, pek_ref, pev_ref, wk_ref, wv_ref, cos_ref, sin_ref,
                       ko_ref, vo_ref, kbuf, vbuf, ksem, vsem, *, n_pages):
    slot = _gather_step(pt_ref, (kc_hbm, vc_hbm), (kbuf, vbuf), (ksem, vsem), n_pages)
    past = n_pages * PAGE_SIZE
    nblk = past // CMP_BLK
    npad = ko_ref.shape[2] - nblk
    n_tok = kn_ref.shape[0] // N_KV

    def tail(new_ref, pe_ref, w_ref, g):
        m = (jnp.sum(new_ref[pl.ds(g, n_tok, stride=N_KV), :], axis=0, keepdims=True)
             + jnp.sum(pe_ref[...], axis=0, keepdims=True)) * (1.0 / CMP_BLK)
        return _dot(jnp.broadcast_to(m, (npad, HEAD_DIM)).astype(BF16), w_ref[...])

    for g in range(N_KV):
        kk = _block_summary(kbuf[slot, pl.ds(g, past, stride=N_KV), :], pek_ref, wk_ref, nblk)
        ko_ref[0, g, 0:nblk, :] = _rope(kk, cos_ref[0:nblk, :], sin_ref[0:nblk, :])
        kt = tail(kn_ref, pek_ref, wk_ref, g)
        ko_ref[0, g, nblk:nblk + npad, :] = _rope(kt, cos_ref[nblk:nblk + npad, :], sin_ref[nblk:nblk + npad, :])
        vo_ref[0, g, 0:nblk, :] = _block_summary(vbuf[slot, pl.ds(g, past, stride=N_KV), :], pev_ref, wv_ref, nblk)
        vo_ref[0, g, nblk:nblk + npad, :] = tail(vn_ref, pev_ref, wv_ref, g)


def _cmp_sample(page_table, cache_k, cache_v, k_new, v_new, pe_k, pe_v, w_k, w_v, cos_e, sin_e, npad):
    nb, n_pages = page_table.shape
    past = n_pages * PAGE_SIZE
    nblk = past // CMP_BLK
    n_tok = k_new.shape[0] // (nb * N_KV)
    full = lambda shape: pl.BlockSpec(shape, lambda b, pt: (0,) * len(shape))
    new = pl.BlockSpec((n_tok * N_KV, HEAD_DIM), lambda b, pt: (b, 0))
    out = pl.BlockSpec((1, N_KV, nblk + npad, HEAD_DIM), lambda b, pt: (b, 0, 0, 0))
    shape = jax.ShapeDtypeStruct((nb, N_KV, nblk + npad, HEAD_DIM), F32)
    return pl.pallas_call(
        functools.partial(_cmp_sample_kernel, n_pages=n_pages),
        grid_spec=pltpu.PrefetchScalarGridSpec(
            num_scalar_prefetch=1,
            grid=(nb,),
            in_specs=[pl.BlockSpec(memory_space=pl.ANY), pl.BlockSpec(memory_space=pl.ANY), new, new,
                      full((CMP_BLK, HEAD_DIM)), full((CMP_BLK, HEAD_DIM)),
                      full((HEAD_DIM, HEAD_DIM)), full((HEAD_DIM, HEAD_DIM)),
                      full((nblk + npad, HEAD_DIM)), full((nblk + npad, HEAD_DIM))],
            out_specs=[out, out],
            scratch_shapes=[pltpu.VMEM((2, past * N_KV, HEAD_DIM), F32), pltpu.VMEM((2, past * N_KV, HEAD_DIM), F32),
                            pltpu.SemaphoreType.DMA((2,)), pltpu.SemaphoreType.DMA((2,))]),
        out_shape=[shape, shape],
        compiler_params=_cparams(("arbitrary",)),
        name="cmp_sample",
    )(page_table, cache_k, cache_v, k_new, v_new, pe_k, pe_v, w_k, w_v, cos_e, sin_e)


def _select_blocks_t(imp_t, cur, nsel):
    nblk = imp_t.shape[0]
    nidx = lax.broadcasted_iota(jnp.int32, imp_t.shape, 0)
    forced = jnp.logical_or(jnp.logical_or(nidx == cur, nidx == cur - 1), nidx == 0)
    imp_t = jnp.where(forced, 1e4, imp_t)
    imp_t = jnp.where(nidx > cur, -1.0, imp_t)
    rank = jnp.zeros(imp_t.shape, F32)
    for i in range(nblk):
        row = imp_t[i:i + 1, :]
        rank = rank + jnp.where(nidx > i, jnp.where(row >= imp_t, 1.0, 0.0), jnp.where(row > imp_t, 1.0, 0.0))
    return jnp.where(rank < float(nsel), 1.0, 0.0)


def _softmax_lanes(s):
    m = jnp.max(s, axis=-1, keepdims=True)
    e = jnp.exp2(s - m)
    return e / jnp.sum(e, axis=-1, keepdims=True)


def _attn_prompt_kernel(q_ref, kc_ref, vc_ref, ks_ref, vs_ref, kw_ref, vw_ref, e_ref, o_ref, *, kt, kv_unroll, nblk, nsel):
    qi = pl.program_id(2)
    tq = q_ref.shape[1]
    nblk_pad = kc_ref.shape[2]
    hh = GQA_R
    nch = GQA_R // hh
    rows = hh * tq
    qs = [q_ref[c * hh:(c + 1) * hh].reshape(rows, HEAD_DIM) for c in range(nch)]
    qpos = qi * tq + lax.broadcasted_iota(jnp.int32, (tq, 1), 0)
    any_valid = jnp.where(qpos >= CMP_BLK - 1, 1.0, 0.0)

    kc = kc_ref[0, 0]
    blk_end = lax.broadcasted_iota(jnp.int32, (1, nblk_pad), 1) * CMP_BLK + (CMP_BLK - 1)
    valid = jnp.logical_and(blk_end <= qpos, blk_end < nblk * CMP_BLK)
    o_cmp = []
    imp = jnp.zeros((tq, nblk_pad), F32)
    for c in range(nch):
        s = jnp.where(valid[None], _dot_nt(qs[c], kc).reshape(hh, tq, nblk_pad), NEG)
        e = jnp.exp2(s - jnp.max(s, axis=-1, keepdims=True))
        inv = any_valid[None] / jnp.sum(e, axis=-1, keepdims=True)
        o_cmp.append(_dot(e.reshape(rows, nblk_pad).astype(BF16), vc_ref[0, 0]) * inv.reshape(rows, 1))
        imp = imp + jnp.sum(e * inv, axis=0)

    qpos_l = qi * tq + lax.broadcasted_iota(jnp.int32, (1, tq), 1)
    sel_t = _select_blocks_t(imp.T[:nblk], qpos_l // CMP_BLK, nsel)
    pad = e_ref.shape[0] - nblk
    if pad:
        sel_t = jnp.concatenate([sel_t, jnp.zeros((pad, tq), F32)], axis=0)
    sel = sel_t.T.astype(BF16)

    def write_heads(branch, outs):
        for c in range(nch):
            for r in range(hh):
                cs = slice((c * hh + r) * HEAD_DIM, (c * hh + r + 1) * HEAD_DIM)
                o_ref[branch, :, cs] = outs[c][r * tq:(r + 1) * tq].astype(BF16)

    span = WINDOW + tq
    start = pl.multiple_of(jnp.maximum(qi * tq - WINDOW, 0), tq)
    kpos_w = start + lax.broadcasted_iota(jnp.int32, (1, span), 1)
    band = jnp.logical_and(kpos_w <= qpos, kpos_w > qpos - WINDOW)
    k_w = kw_ref[0, pl.ds(start, span), :]
    v_w = vw_ref[0, pl.ds(start, span), :]
    o_win = []
    for c in range(nch):
        sw = jnp.where(band[None], _dot_nt(qs[c], k_w).reshape(hh, tq, span), NEG)
        ew = jnp.exp2(sw - jnp.max(sw, axis=-1, keepdims=True))
        inv_w = 1.0 / jnp.sum(ew, axis=-1, keepdims=True)
        o_win.append(_dot(ew.reshape(rows, span).astype(BF16), v_w) * inv_w.reshape(rows, 1))
    write_heads(0, o_cmp)
    write_heads(2, o_win)

    def kv_tile(j, carry):
        off = pl.multiple_of(j * kt, kt)
        kpos = off + lax.broadcasted_iota(jnp.int32, (1, kt), 1)
        hit = _dot(sel, e_ref[:, pl.ds(off, kt)])
        bias = jnp.where(kpos <= qpos, (hit - 1.0) * (-NEG), NEG)
        k_t = ks_ref[0, pl.ds(off, kt), :]
        v_t = vs_ref[0, pl.ds(off, kt), :]
        out = []
        for c in range(nch):
            m_i, l_i, acc = carry[3 * c:3 * c + 3]
            sj = _dot_nt(qs[c], k_t).reshape(hh, tq, kt) + bias[None]
            m_new = jnp.maximum(m_i, jnp.max(sj, axis=-1, keepdims=True))
            alpha = jnp.exp2(m_i - m_new)
            pj = jnp.exp2(sj - m_new)
            l_new = alpha * l_i + jnp.sum(pj, axis=-1, keepdims=True)
            pv = _dot(pj.reshape(rows, kt).astype(BF16), v_t)
            out += [m_new, l_new, acc * alpha.reshape(rows, 1) + pv]
        return tuple(out)

    def kv_step(jj, carry):
        for u in range(kv_unroll):
            carry = kv_tile(jj * kv_unroll + u, carry)
        return carry

    n_tiles = (qi * tq + tq + kt - 1) // kt
    init = (jnp.full((hh, tq, 1), NEG, F32), jnp.zeros((hh, tq, 1), F32), jnp.zeros((rows, HEAD_DIM), F32)) * nch
    fin = lax.fori_loop(0, n_tiles // kv_unroll, kv_step, init)
    for u in range(kv_unroll - 1):
        j_left = (n_tiles // kv_unroll) * kv_unroll + u
        fin = lax.cond(j_left < n_tiles, functools.partial(kv_tile, j_left), lambda c: c, fin)
    o_sel = [fin[3 * c + 2] / fin[3 * c + 1].reshape(rows, 1) for c in range(nch)]

    write_heads(1, o_sel)


def _attn_prompt(q_hm, kcmp, vcmp, kv_hm, expand, nbatch, seq, tq, kt):
    nq = seq // tq
    nblk = seq // CMP_BLK
    nblk_pad = expand.shape[0]
    kcmp, vcmp = (jnp.pad(a, ((0, 0), (0, 0), (0, nblk_pad - nblk), (0, 0))) for a in (kcmp, vcmp))
    kv = pl.BlockSpec((1, seq, HEAD_DIM), lambda b, g, i: (g, b, 0))
    cmp = pl.BlockSpec((1, 1, nblk_pad, HEAD_DIM), lambda b, g, i: (b, g, 0, 0))
    return pl.pallas_call(
        functools.partial(_attn_prompt_kernel, kt=kt, kv_unroll=2, nblk=nblk, nsel=min(N_SEL, nblk)),
        grid=(nbatch, N_KV, nq),
        in_specs=[pl.BlockSpec((GQA_R, tq, HEAD_DIM), lambda b, g, i: (g, b * nq + i, 0)),
                  cmp, cmp, kv, kv, kv, kv,
                  pl.BlockSpec(expand.shape, lambda b, g, i: (0, 0))],
        out_specs=pl.BlockSpec((3, tq, GQA_R * HEAD_DIM), lambda b, g, i: (0, b * nq + i, g)),
        out_shape=jax.ShapeDtypeStruct((3, nbatch * seq, D_ATT), BF16),
        compiler_params=_cparams(("parallel", "parallel", "arbitrary")),
        name="attn_prompt",
    )(q_hm, kcmp, vcmp, *kv_hm, expand)


def _attn_sample_kernel(pt_ref, ks_hbm, vs_hbm, q_ref, kc_ref, vc_ref, ksn_ref, vsn_ref, kwc_ref, vwc_ref,
                        kwn_ref, vwn_ref, e_ref, o_ref, okw_ref, ovw_ref, kbuf, vbuf, newbuf, ksem, vsem,
                        *, n_pages, n_tok, nsel):
    b = pl.program_id(0)
    slot = _gather_step(pt_ref, (ks_hbm, vs_hbm), (kbuf, vbuf), (ksem, vsem), n_pages)
    past = n_pages * PAGE_SIZE
    nblk_c = past // CMP_BLK
    nblk = kc_ref.shape[2]
    rows = GQA_R * n_tok
    lw = kwc_ref.shape[0] // N_KV
    nr = n_tok * N_KV

    @pl.when(b == 0)
    def _():
        newbuf[...] = jnp.zeros(newbuf.shape, F32)

    for i, ref in enumerate((ksn_ref, vsn_ref, kwn_ref, vwn_ref)):
        newbuf[i, 0:nr, :] = ref[...]

    okw_ref[0:(lw - n_tok) * N_KV, :] = kwc_ref[nr:lw * N_KV, :]
    okw_ref[(lw - n_tok) * N_KV:lw * N_KV, :] = kwn_ref[...]
    ovw_ref[0:(lw - n_tok) * N_KV, :] = vwc_ref[nr:lw * N_KV, :]
    ovw_ref[(lw - n_tok) * N_KV:lw * N_KV, :] = vwn_ref[...]

    rows_all = N_KV * rows
    t_row = lax.broadcasted_iota(jnp.int32, (rows_all, 1), 0) % n_tok
    qpos = past + t_row
    nidx = lax.broadcasted_iota(jnp.int32, (rows_all, nblk), 1)
    valid = nidx * CMP_BLK + (CMP_BLK - 1) <= qpos
    cur = qpos // CMP_BLK
    forced = jnp.logical_or(jnp.logical_or(nidx == cur, nidx == cur - 1), nidx == 0)
    new_lane = lax.broadcasted_iota(jnp.int32, (1, PAGE_SIZE), 1)
    new_ok = jnp.logical_and(new_lane < n_tok, past + new_lane <= qpos)
    wpos = (past - lw) + lax.broadcasted_iota(jnp.int32, (1, lw), 1)
    win_ok = jnp.logical_and(wpos <= qpos, wpos > qpos - WINDOW)
    win_new_ok = jnp.logical_and(new_ok, past + new_lane > qpos - WINDOW)

    groups = range(N_KV)
    grp = lambda x, g: x[g * rows:(g + 1) * rows]
    stack = lambda f: jnp.concatenate([f(g) for g in groups], axis=0)
    group_rows = lambda ref, n: (lambda g: ref[pl.ds(g, n, stride=N_KV), :].astype(BF16))
    new_rows = lambda i: (lambda g: newbuf[i, pl.ds(g, PAGE_SIZE, stride=N_KV), :].astype(BF16))
    qs = [q_ref[g, 0].astype(BF16) for g in groups]

    def attend(s_parts, v_parts):
        ps = _softmax_lanes(jnp.concatenate(s_parts, axis=1)).astype(BF16)
        outs = []
        for g in groups:
            pg, off, acc = grp(ps, g), 0, None
            for sp, vp in zip(s_parts, v_parts):
                d = _dot(pg[:, off:off + sp.shape[1]], vp(g))
                acc = d if acc is None else acc + d
                off += sp.shape[1]
            outs.append(acc)
        return outs

    p = _softmax_lanes(jnp.where(valid, stack(lambda g: _dot_nt(qs[g], kc_ref[0, g].astype(BF16))), NEG))
    p = p * jnp.where(qpos >= CMP_BLK - 1, 1.0, 0.0)
    o_cmp = [_dot(grp(p, g).astype(BF16), vc_ref[0, g].astype(BF16)) for g in groups]

    def head_sum(pg):
        u = pg + pltpu.roll(pg, (GQA_R // 2) * n_tok, 0)
        return u + pltpu.roll(u, n_tok, 0)

    imp = stack(lambda g: head_sum(grp(p, g)))
    imp = jnp.where(forced, 1e4, imp)
    imp = jnp.where(nidx > cur, -1.0, imp)
    rank = jnp.zeros((rows_all, nblk), F32)
    for i in range(nblk):
        col = imp[:, i:i + 1]
        rank = rank + jnp.where(nidx > i, jnp.where(col >= imp, 1.0, 0.0), jnp.where(col > imp, 1.0, 0.0))
    sel = jnp.where(rank < float(nsel), 1.0, 0.0)

    hit = _dot(sel.astype(BF16), e_ref[...])
    k_cached = lambda g: kbuf[slot, pl.ds(g, past, stride=N_KV), :].astype(BF16)
    v_cached = lambda g: vbuf[slot, pl.ds(g, past, stride=N_KV), :].astype(BF16)
    s_c = jnp.where(hit > 0.5, stack(lambda g: _dot_nt(qs[g], k_cached(g))), NEG)
    new_sel = jnp.logical_and(new_ok, sel[:, nblk_c:nblk_c + 1] > 0.5)
    s_n = jnp.where(new_sel, stack(lambda g: _dot_nt(qs[g], new_rows(0)(g))), NEG)
    o_sel = attend([s_c, s_n], [v_cached, new_rows(1)])

    s_w = jnp.where(win_ok, stack(lambda g: _dot_nt(qs[g], group_rows(kwc_ref, lw)(g))), NEG)
    s_wn = jnp.where(win_new_ok, stack(lambda g: _dot_nt(qs[g], new_rows(2)(g))), NEG)
    o_win = attend([s_w, s_wn], [group_rows(vwc_ref, lw), new_rows(3)])

    for j, branch in enumerate((o_cmp, o_sel, o_win)):
        for g in groups:
            for r in range(GQA_R):
                h = g * GQA_R + r
                for t in range(n_tok):
                    o_ref[j, 0, t:t + 1, h * HEAD_DIM:(h + 1) * HEAD_DIM] = branch[g][r * n_tok + t:r * n_tok + t + 1, :]


def _attn_sample(page_table, cache_ks, cache_vs, q_s, kcmp, vcmp, ks_new, vs_new, cache_kw, cache_vw, kw_new, vw_new,
                 expand):
    nb, n_pages = page_table.shape
    past = n_pages * PAGE_SIZE
    n_tok = ks_new.shape[0] // (nb * N_KV)
    rows = GQA_R * n_tok
    nblk = kcmp.shape[2]
    lw = cache_kw.shape[0] // (nb * N_KV)
    new = pl.BlockSpec((n_tok * N_KV, HEAD_DIM), lambda b, pt: (b, 0))
    cmp = pl.BlockSpec((1, N_KV, nblk, HEAD_DIM), lambda b, pt: (b, 0, 0, 0))
    win = pl.BlockSpec((lw * N_KV, HEAD_DIM), lambda b, pt: (b, 0))
    any_ = pl.BlockSpec(memory_space=pl.ANY)
    return pl.pallas_call(
        functools.partial(_attn_sample_kernel, n_pages=n_pages, n_tok=n_tok, nsel=min(N_SEL, past // CMP_BLK + 1)),
        grid_spec=pltpu.PrefetchScalarGridSpec(
            num_scalar_prefetch=1,
            grid=(nb,),
            in_specs=[any_, any_,
                      pl.BlockSpec((N_KV, 1, rows, HEAD_DIM), lambda b, pt: (0, b, 0, 0)),
                      cmp, cmp, new, new, win, win, new, new,
                      pl.BlockSpec(expand.shape, lambda b, pt: (0, 0))],
            out_specs=[pl.BlockSpec((3, 1, n_tok, D_ATT), lambda b, pt: (0, b, 0, 0)), win, win],
            scratch_shapes=[pltpu.VMEM((2, past * N_KV, HEAD_DIM), F32), pltpu.VMEM((2, past * N_KV, HEAD_DIM), F32),
                            pltpu.VMEM((4, PAGE_SIZE * N_KV, HEAD_DIM), F32),
                            pltpu.SemaphoreType.DMA((2,)), pltpu.SemaphoreType.DMA((2,))]),
        out_shape=[jax.ShapeDtypeStruct((3, nb, n_tok, D_ATT), F32),
                   jax.ShapeDtypeStruct(cache_kw.shape, F32), jax.ShapeDtypeStruct(cache_vw.shape, F32)],
        compiler_params=_cparams(("arbitrary",)),
        name="attn_sample",
    )(page_table, cache_ks, cache_vs, q_s, kcmp, vcmp, ks_new, vs_new, cache_kw, cache_vw, kw_new, vw_new, expand)


def _expand_matrix(rows, nkeys):
    return (jnp.arange(rows, dtype=jnp.int32)[:, None] == (jnp.arange(nkeys, dtype=jnp.int32) // CMP_BLK)[None, :]).astype(BF16)


def _pick(m, pref):
    while m % pref:
        pref //= 2
    return pref


def kernel(x_prompt, x_sample, cache_k_cmp, cache_v_cmp, cache_k_sel, cache_v_sel, cache_k_win, cache_v_win, state_h, state_conv, page_table, norm_mix_pre, w_in, conv_w, conv_b, w_ga, b_ga, w_gx, b_gx, lam, pe_k, pe_v, w_phi_k, w_phi_v, norm_rnn_out, norm_att_out, w_out, norm_mix_post, norm_mlp_pre, w_up, w_down, norm_mlp_post):
    depth = w_in.shape[0]
    assert depth == 1, "single layer only"
    bp, seq, d_model = x_prompt.shape
    bs, n_tok, _ = x_sample.shape
    d_rnn = conv_w.shape[-1]
    n_pages = page_table.shape[1]
    past = n_pages * PAGE_SIZE
    lw_in = cache_k_win.shape[2]
    assert lw_in == WINDOW and past % CMP_BLK == 0 and n_tok <= CMP_BLK and d_rnn == RNN_BLOCKS * RNN_BW
    l = 0

    w = w_in[l].astype(BF16)
    c0 = 2 * d_rnn
    q_blk0 = c0 // (GQA_R * HEAD_DIM)
    kv_blk0 = (c0 + D_ATT) // D_KV
    w_gl = jnp.pad(w[:, c0 + D_ATT + 6 * D_KV:], ((0, 0), (0, LANES - 3 * N_HEADS)))
    w_o = w_out[l].astype(BF16)
    w_u = w_up[l].astype(BF16)
    w_d = w_down[l].astype(BF16)
    wg = jnp.concatenate([w_ga[l], w_gx[l]], axis=-1).astype(BF16)
    wk_phi = w_phi_k[l].astype(BF16)
    wv_phi = w_phi_v[l].astype(BF16)
    vec = lambda a: a[l].reshape(1, -1)
    rnn_p = (conv_w[l], vec(conv_b), wg, vec(b_ga), vec(b_gx), vec(lam))

    def mixer_tail(x2d, o_rnn, o3, gates):
        cat = _cat_norm(o_rnn, o3, gates, vec(norm_rnn_out), vec(norm_att_out), _pick(x2d.shape[0], 256))
        tm_t = _pick(x2d.shape[0], 512)
        h, hn = _out_proj(cat, w_o, x2d, vec(norm_mix_post), vec(norm_mlp_pre), _pick(x2d.shape[0], 256), w_o.shape[1])
        return _mlp(hn, w_u, w_d, h, vec(norm_mlp_post), tm_t, 512)

    mp = bp * seq
    xp2 = x_prompt.reshape(mp, d_model)
    tm_p = _pick(seq, 1024)
    cos_p, sin_p = _rope_tables(jnp.arange(seq, dtype=jnp.int32))
    xn = _norm_cast(xp2, vec(norm_mix_pre), _pick(mp, 512))
    xy = _matmul(xn, w, tm_p, 512, name="proj_xy", n=c0)
    q_hm = _proj_q_prompt(xn, w, q_blk0, cos_p, sin_p, tm_p)
    kv_p = [_proj_kv(xn, w, kv_blk0 + s, cos_p, sin_p, tm_p, KV_ROPE[s], s >= 2) for s in range(6)]
    kv_nat = [o[0] for o in kv_p]
    kv_hm = [o[1] for o in kv_p[2:]]
    gates = _matmul(xn, w_gl, tm_p, LANES, body=_mm_gate_kernel, name="proj_gate")
    o_rnn, h_p, cv_p = _rglru_prompt(xy, *rnn_p, bp, seq, _pick(seq, 128))
    nblk_p = seq // CMP_BLK
    cos_e, sin_e = _rope_tables(jnp.arange(nblk_p, dtype=jnp.int32) * CMP_BLK + CMP_BLK - 1)
    cmp_rows = _pick(seq, 1024)
    kcmp, vcmp = _cmp_prompt(kv_nat[0], kv_nat[1], pe_k[l], pe_v[l], wk_phi, wv_phi, cos_e, sin_e, bp, seq, cmp_rows)
    e_rows = max(LANES, nblk_p)
    o3 = _attn_prompt(q_hm, kcmp, vcmp, kv_hm, _expand_matrix(e_rows, seq), bp, seq, 256, _pick(seq, 512))
    y_p = mixer_tail(xp2, o_rnn, o3, gates).reshape(bp, seq, d_model)
    lw_p = min(WINDOW, seq)
    nat5 = lambda a: a.reshape(1, bp, seq, N_KV, HEAD_DIM)
    p_out = (nat5(kv_nat[0]), nat5(kv_nat[1]), nat5(kv_nat[2]), nat5(kv_nat[3]),
             nat5(kv_nat[4])[:, :, seq - lw_p:], nat5(kv_nat[5])[:, :, seq - lw_p:],
             h_p.reshape(1, bp, d_rnn), cv_p.reshape(1, bp, CONV_W - 1, d_rnn))

    ms = bs * n_tok
    xs_tb = x_sample.swapaxes(0, 1).reshape(ms, d_model)
    pos_s = past + jnp.arange(n_tok, dtype=jnp.int32)
    cos_s, sin_s = _rope_tables(jnp.repeat(pos_s, bs))
    xn_s = _norm_cast(xs_tb, vec(norm_mix_pre), ms)
    xy_s = _matmul(xn_s, w, ms, 512, name="proj_xy", n=c0)
    q_s = _proj_q_sample(xn_s, w, q_blk0, cos_s, sin_s, n_tok)
    kv_s = [_proj_kv(xn_s, w, kv_blk0 + s, cos_s, sin_s, ms, KV_ROPE[s], False)[0] for s in range(6)]
    gates_s = _matmul(xn_s, w_gl, ms, LANES, body=_mm_gate_kernel, name="proj_gate")
    o_rnn_s, h_s, cv_s = _rglru_sample(xy_s, state_conv[l].swapaxes(0, 1), state_h[l], *rnn_p, n_tok, past)
    kv_bt = [a.reshape(n_tok, bs, N_KV, HEAD_DIM).swapaxes(0, 1) for a in kv_s]
    kv_s2 = [a.reshape(ms * N_KV, HEAD_DIM) for a in kv_bt]
    rows2d = lambda c: c.reshape(-1, HEAD_DIM)
    nblk_c = past // CMP_BLK
    npad = 8
    cos_es, sin_es = _rope_tables(jnp.arange(nblk_c + npad, dtype=jnp.int32) * CMP_BLK + CMP_BLK - 1)
    kcmp_s, vcmp_s = _cmp_sample(page_table, rows2d(cache_k_cmp), rows2d(cache_v_cmp), kv_s2[0], kv_s2[1],
                                 pe_k[l], pe_v[l], wk_phi, wv_phi, cos_es, sin_es, npad)
    o3_s, kw_out, vw_out = _attn_sample(page_table, rows2d(cache_k_sel), rows2d(cache_v_sel), q_s, kcmp_s, vcmp_s,
                                        kv_s2[2], kv_s2[3], rows2d(cache_k_win), rows2d(cache_v_win), kv_s2[4], kv_s2[5],
                                        _expand_matrix(nblk_c + npad, past))
    gates_bt = gates_s.reshape(n_tok, bs, LANES).swapaxes(0, 1).reshape(ms, LANES)
    y_s = mixer_tail(x_sample.reshape(ms, d_model), o_rnn_s.reshape(ms, d_rnn), o3_s.reshape(3, ms, D_ATT),
                     gates_bt).reshape(bs, n_tok, d_model)
    new5 = lambda a: a.reshape(1, bs, n_tok, N_KV, HEAD_DIM)
    s_out = (new5(kv_bt[0]), new5(kv_bt[1]), new5(kv_bt[2]), new5(kv_bt[3]),
             kw_out.reshape(cache_k_win.shape), vw_out.reshape(cache_v_win.shape),
             h_s.reshape(1, bs, d_rnn), cv_s.swapaxes(0, 1).reshape(1, bs, CONV_W - 1, d_rnn))
    return (y_p, y_s) + p_out + s_out
```

```python
import functools

import jax
import jax.numpy as jnp
import numpy as np
from jax import lax
from jax.experimental import pallas as pl
from jax.experimental.pallas import tpu as pltpu

F32 = jnp.float32
BF16 = jnp.bfloat16

HEAD_DIM = 128
N_KV = 4
GQA_R = 4
N_HEADS = N_KV * GQA_R
D_ATT = N_HEADS * HEAD_DIM
D_KV = N_KV * HEAD_DIM
CMP_BLK = 64
N_SEL = 16
WINDOW = 512
CONV_W = 4
C_GATE = 8.0
RNN_BLOCKS = 16
RNN_BW = 128
ROPE_THETA = 10000.0
EPS = 1e-6
NEG = -1e30
PAGE_SIZE = 128
SCALE = HEAD_DIM ** -0.5
LOG2E = 1.4426950408889634
QSCALE = SCALE * LOG2E
LANES = 128
VMEM_LIMIT = 56 * 1024 * 1024


def _cparams(sem):
    return pltpu.CompilerParams(dimension_semantics=sem, vmem_limit_bytes=VMEM_LIMIT)


def _rope_tables(pos):
    half = HEAD_DIM // 2
    inv = ROPE_THETA ** (-jnp.arange(half, dtype=F32) * (2.0 / HEAD_DIM))
    ang = pos.astype(F32)[:, None] * inv[None, :]
    c, s = jnp.cos(ang), jnp.sin(ang)
    return jnp.concatenate([c, c], axis=-1), jnp.concatenate([-s, s], axis=-1)


def _rope(x, cos2, sin2):
    return x * cos2 + pltpu.roll(x, HEAD_DIM // 2, 1) * sin2


def _rms(x):
    return x * lax.rsqrt(jnp.mean(x * x, axis=-1, keepdims=True) + EPS)


def _gelu(x):
    return 0.5 * x * (1.0 + jnp.tanh(0.7978845608028654 * (x + 0.044715 * (x * x * x))))


def _softplus(x):
    return jnp.maximum(x, 0.0) + jnp.log1p(jnp.exp(-jnp.abs(x)))


def _sigmoid(x):
    return 1.0 / (1.0 + jnp.exp(-x))


def _norm_kernel(x_ref, g_ref, o_ref):
    o_ref[...] = (_rms(x_ref[...]) * g_ref[...]).astype(o_ref.dtype)


def _norm_cast(x, g, tm):
    m, d = x.shape
    return pl.pallas_call(
        _norm_kernel,
        grid=(m // tm,),
        in_specs=[pl.BlockSpec((tm, d), lambda i: (i, 0)), pl.BlockSpec((1, d), lambda i: (0, 0))],
        out_specs=pl.BlockSpec((tm, d), lambda i: (i, 0)),
        out_shape=jax.ShapeDtypeStruct((m, d), BF16),
        compiler_params=_cparams(("parallel",)),
        name="norm_cast",
    )(x, g)


def _cat_norm_kernel(orn_ref, o3_ref, gate_ref, grn_ref, gat_ref, cat_ref, tmp):
    d = orn_ref.shape[1]
    cat_ref[:, :d] = (_rms(orn_ref[...].astype(F32)) * grn_ref[...]).astype(BF16)
    g = gate_ref[...]
    ss = jnp.zeros((g.shape[0], 1), F32)
    for h in range(N_HEADS):
        sl = slice(h * HEAD_DIM, (h + 1) * HEAD_DIM)
        oh = (g[:, 3 * h:3 * h + 1] * o3_ref[0, :, sl].astype(F32)
              + g[:, 3 * h + 1:3 * h + 2] * o3_ref[1, :, sl].astype(F32)
              + g[:, 3 * h + 2:3 * h + 3] * o3_ref[2, :, sl].astype(F32))
        tmp[:, sl] = oh
        ss = ss + jnp.sum(oh * oh, axis=-1, keepdims=True)
    inv = lax.rsqrt(ss * (1.0 / D_ATT) + EPS)
    cat_ref[:, d:] = (tmp[...] * inv * gat_ref[...]).astype(BF16)


def _pair_norm_kernel(a_ref, b_ref, ga_ref, gb_ref, cat_ref):
    d = a_ref.shape[1]
    cat_ref[:, :d] = (_rms(a_ref[...].astype(F32)) * ga_ref[...]).astype(BF16)
    cat_ref[:, d:] = (_rms(b_ref[...].astype(F32)) * gb_ref[...]).astype(BF16)


def _pair_norm(a, b, ga, gb, tm):
    m, da = a.shape
    db = b.shape[1]
    return pl.pallas_call(
        _pair_norm_kernel,
        grid=(m // tm,),
        in_specs=[pl.BlockSpec((tm, da), lambda i: (i, 0)), pl.BlockSpec((tm, db), lambda i: (i, 0)),
                  pl.BlockSpec((1, da), lambda i: (0, 0)), pl.BlockSpec((1, db), lambda i: (0, 0))],
        out_specs=pl.BlockSpec((tm, da + db), lambda i: (i, 0)),
        out_shape=jax.ShapeDtypeStruct((m, da + db), BF16),
        compiler_params=_cparams(("parallel",)),
        name="pair_norm",
    )(a, b, ga, gb)


def _cat_norm(o_rnn, o3, gates, g_rnn, g_att, tm):
    m, d = o_rnn.shape
    return pl.pallas_call(
        _cat_norm_kernel,
        grid=(m // tm,),
        in_specs=[pl.BlockSpec((tm, d), lambda i: (i, 0)),
                  pl.BlockSpec((3, tm, D_ATT), lambda i: (0, i, 0)),
                  pl.BlockSpec((tm, LANES), lambda i: (i, 0)),
                  pl.BlockSpec((1, d), lambda i: (0, 0)),
                  pl.BlockSpec((1, D_ATT), lambda i: (0, 0))],
        out_specs=pl.BlockSpec((tm, d + D_ATT), lambda i: (i, 0)),
        out_shape=jax.ShapeDtypeStruct((m, d + D_ATT), BF16),
        scratch_shapes=[pltpu.VMEM((tm, D_ATT), F32)],
        compiler_params=_cparams(("parallel",)),
        name="cat_norm",
    )(o_rnn, o3, gates, g_rnn, g_att)


def _dot(a, b):
    return jnp.dot(a, b, preferred_element_type=F32)


def _dot_nt(a, b):
    return lax.dot_general(a, b, (((1,), (1,)), ((), ())), preferred_element_type=F32)


def _mm_plain_kernel(a_ref, b_ref, o_ref):
    o_ref[...] = _dot(a_ref[...], b_ref[...])


def _mm_gate_kernel(a_ref, b_ref, o_ref):
    o_ref[...] = _sigmoid(_dot(a_ref[...], b_ref[...]))


def _matmul(a, b, tm, tn, body=_mm_plain_kernel, name="matmul", n=None, col0=0):
    m, k = a.shape
    n = b.shape[1] if n is None else n
    return pl.pallas_call(
        body,
        grid=(m // tm, n // tn),
        in_specs=[pl.BlockSpec((tm, k), lambda i, j: (i, 0)), pl.BlockSpec((k, tn), lambda i, j: (0, col0 + j))],
        out_specs=pl.BlockSpec((tm, tn), lambda i, j: (i, j)),
        out_shape=jax.ShapeDtypeStruct((m, n), F32),
        compiler_params=_cparams(("parallel", "arbitrary")),
        name=name,
    )(a, b)


def _mm_q_prompt_kernel(a_ref, b_ref, cos_ref, sin_ref, o_ref):
    acc = _dot(a_ref[...], b_ref[...])
    cos2, sin2 = cos_ref[...], sin_ref[...]
    for r in range(GQA_R):
        x = acc[:, r * HEAD_DIM:(r + 1) * HEAD_DIM]
        o_ref[r] = (_rope(x, cos2, sin2) * QSCALE).astype(BF16)


def _proj_q_prompt(a, w, col0, cos2, sin2, tm):
    m, k = a.shape
    nt = cos2.shape[0] // tm
    return pl.pallas_call(
        _mm_q_prompt_kernel,
        grid=(m // tm, N_KV),
        in_specs=[pl.BlockSpec((tm, k), lambda i, j: (i, 0)),
                  pl.BlockSpec((k, GQA_R * HEAD_DIM), lambda i, j: (0, col0 + j)),
                  pl.BlockSpec((tm, HEAD_DIM), lambda i, j: (i % nt, 0)),
                  pl.BlockSpec((tm, HEAD_DIM), lambda i, j: (i % nt, 0))],
        out_specs=pl.BlockSpec((GQA_R, tm, HEAD_DIM), lambda i, j: (j, i, 0)),
        out_shape=jax.ShapeDtypeStruct((N_HEADS, m, HEAD_DIM), BF16),
        compiler_params=_cparams(("parallel", "arbitrary")),
        name="proj_q_prompt",
    )(a, w, cos2, sin2)


def _mm_q_sample_kernel(a_ref, b_ref, cos_ref, sin_ref, o_ref, tmp, *, n_tok):
    acc = _dot(a_ref[...], b_ref[...])
    cos2, sin2 = cos_ref[...], sin_ref[...]
    nb = o_ref.shape[1]
    for r in range(GQA_R):
        x = acc[:, r * HEAD_DIM:(r + 1) * HEAD_DIM]
        tmp[...] = _rope(x, cos2, sin2) * QSCALE
        for t in range(n_tok):
            o_ref[0, :, n_tok * r + t, :] = tmp[t * nb:(t + 1) * nb, :]


def _proj_q_sample(a, w, col0, cos2, sin2, n_tok):
    m, k = a.shape
    nb = m // n_tok
    return pl.pallas_call(
        functools.partial(_mm_q_sample_kernel, n_tok=n_tok),
        grid=(N_KV,),
        in_specs=[pl.BlockSpec((m, k), lambda j: (0, 0)),
                  pl.BlockSpec((k, GQA_R * HEAD_DIM), lambda j: (0, col0 + j)),
                  pl.BlockSpec((m, HEAD_DIM), lambda j: (0, 0)),
                  pl.BlockSpec((m, HEAD_DIM), lambda j: (0, 0))],
        out_specs=pl.BlockSpec((1, nb, GQA_R * n_tok, HEAD_DIM), lambda j: (j, 0, 0, 0)),
        out_shape=jax.ShapeDtypeStruct((N_KV, nb, GQA_R * n_tok, HEAD_DIM), F32),
        scratch_shapes=[pltpu.VMEM((m, HEAD_DIM), F32)],
        compiler_params=_cparams(("arbitrary",)),
        name="proj_q_sample",
    )(a, w, cos2, sin2)


def _mm_kv_kernel(a_ref, b_ref, *refs, rope, with_hm):
    tm = a_ref.shape[0]
    acc = _dot(a_ref[...], b_ref[...])
    outs = refs[2:] if rope else refs
    for g in range(N_KV):
        y = acc[:, g * HEAD_DIM:(g + 1) * HEAD_DIM]
        if rope:
            y = _rope(y, refs[0][...], refs[1][...])
        outs[0][pl.ds(g, tm, stride=N_KV), :] = y
        if with_hm:
            outs[1][g] = y.astype(BF16)


def _proj_kv(a, w, col, cos2, sin2, tm, rope, with_hm):
    m, k = a.shape
    nt = cos2.shape[0] // tm
    tab = pl.BlockSpec((tm, HEAD_DIM), lambda i: (i % nt, 0))
    out_specs = [pl.BlockSpec((tm * N_KV, HEAD_DIM), lambda i: (i, 0))]
    out_shape = [jax.ShapeDtypeStruct((m * N_KV, HEAD_DIM), F32)]
    if with_hm:
        out_specs.append(pl.BlockSpec((N_KV, tm, HEAD_DIM), lambda i: (0, i, 0)))
        out_shape.append(jax.ShapeDtypeStruct((N_KV, m, HEAD_DIM), BF16))
    return pl.pallas_call(
        functools.partial(_mm_kv_kernel, rope=rope, with_hm=with_hm),
        grid=(m // tm,),
        in_specs=[pl.BlockSpec((tm, k), lambda i: (i, 0)), pl.BlockSpec((k, D_KV), lambda i: (0, col))]
                 + ([tab, tab] if rope else []),
        out_specs=out_specs,
        out_shape=out_shape,
        compiler_params=_cparams(("parallel",)),
        name="proj_kv",
    )(a, w, *((cos2, sin2) if rope else ()))


KV_ROPE = (False, False, True, False, True, False)


ROW_CHUNK = 64


def _residual_copy(res_hbm, rbuf, sem):
    tm = rbuf.shape[0]
    return pltpu.make_async_copy(res_hbm.at[pl.ds(pl.multiple_of(pl.program_id(0) * tm, tm), tm)], rbuf, sem)


def _mlp_kernel(h_ref, wu_ref, wd_ref, res_hbm, g_ref, o_ref, rbuf, sem):
    j = pl.program_id(1)

    @pl.when(j == 0)
    def _():
        _residual_copy(res_hbm, rbuf, sem).start()
        o_ref[...] = jnp.zeros(o_ref.shape, F32)

    u = _dot(h_ref[...], wu_ref[...])
    f = jnp.square(jnp.maximum(u, 0.0)).astype(BF16)
    o_ref[...] += _dot(f, wd_ref[...])

    @pl.when(j == pl.num_programs(1) - 1)
    def _():
        _residual_copy(res_hbm, rbuf, sem).wait()
        for r in range(0, o_ref.shape[0], ROW_CHUNK):
            rows = slice(r, r + ROW_CHUNK)
            o_ref[rows, :] = rbuf[rows, :] + _rms(o_ref[rows, :]) * g_ref[...]


def _mlp(hn, w_up, w_down, res, g, tm, tf):
    m, d = hn.shape
    f = w_up.shape[1]
    return pl.pallas_call(
        _mlp_kernel,
        grid=(m // tm, f // tf),
        in_specs=[pl.BlockSpec((tm, d), lambda i, j: (i, 0)),
                  pl.BlockSpec((d, tf), lambda i, j: (0, j)),
                  pl.BlockSpec((tf, d), lambda i, j: (j, 0)),
                  pl.BlockSpec(memory_space=pl.ANY),
                  pl.BlockSpec((1, d), lambda i, j: (0, 0))],
        out_specs=pl.BlockSpec((tm, d), lambda i, j: (i, 0)),
        out_shape=jax.ShapeDtypeStruct((m, d), F32),
        scratch_shapes=[pltpu.VMEM((tm, d), F32), pltpu.SemaphoreType.DMA(())],
        compiler_params=_cparams(("arbitrary", "arbitrary")),
        name="mlp",
    )(hn, w_up, w_down, res, g)


def _out_proj_kernel(a_ref, b_ref, res_hbm, g_ref, g2_ref, h_ref, hn_ref, rbuf, sem):
    j = pl.program_id(1)
    tn = b_ref.shape[1]

    @pl.when(j == 0)
    def _():
        _residual_copy(res_hbm, rbuf, sem).start()

    h_ref[:, pl.ds(pl.multiple_of(j * tn, tn), tn)] = _dot(a_ref[...], b_ref[...])

    @pl.when(j == pl.num_programs(1) - 1)
    def _():
        _residual_copy(res_hbm, rbuf, sem).wait()
        for r in range(0, h_ref.shape[0], ROW_CHUNK):
            rows = slice(r, r + ROW_CHUNK)
            y = rbuf[rows, :] + _rms(h_ref[rows, :]) * g_ref[...]
            h_ref[rows, :] = y
            hn_ref[rows, :] = (_rms(y) * g2_ref[...]).astype(BF16)


def _out_proj(a, b, res, g, g2, tm, tn):
    m, k = a.shape
    n = b.shape[1]
    row = pl.BlockSpec((tm, n), lambda i, j: (i, 0))
    vec = pl.BlockSpec((1, n), lambda i, j: (0, 0))
    b_mode = {"pipeline_mode": pl.Buffered(1)} if tn == n else {}
    return pl.pallas_call(
        _out_proj_kernel,
        grid=(m // tm, n // tn),
        in_specs=[pl.BlockSpec((tm, k), lambda i, j: (i, 0)), pl.BlockSpec((k, tn), lambda i, j: (0, j), **b_mode),
                  pl.BlockSpec(memory_space=pl.ANY), vec, vec],
        out_specs=[row, row],
        out_shape=[jax.ShapeDtypeStruct((m, n), F32), jax.ShapeDtypeStruct((m, n), BF16)],
        scratch_shapes=[pltpu.VMEM((tm, n), F32), pltpu.SemaphoreType.DMA(())],
        compiler_params=_cparams(("arbitrary", "arbitrary")),
        name="out_proj",
    )(a, b, res, g, g2)


def _rglru_gates(xc, gg, bga, bgx, lam, first_row):
    ra = _sigmoid(gg[:, :RNN_BW] + bga)
    rx = _sigmoid(gg[:, RNN_BW:] + bgx)
    log_a = (-C_GATE) * ra * _softplus(-lam)
    a = jnp.exp(log_a)
    th = jnp.tanh(log_a)
    mult = jnp.sqrt((-2.0) * th / (1.0 - th))
    if first_row is not None:
        mult = jnp.where(first_row, 1.0, mult)
    return a, mult * rx * xc


def _rglru_prompt_kernel(xr_ref, yr_ref, cw_ref, cb_ref, wg_ref, bga_ref, bgx_ref, lam_ref,
                         o_ref, h_ref, conv_ref, xbuf, hcar):
    ti = pl.program_id(1)
    tt = xr_ref.shape[0]

    @pl.when(ti == 0)
    def _():
        xbuf[0:8, :] = jnp.zeros((8, xbuf.shape[1]), F32)
        hcar[...] = jnp.zeros(hcar.shape, F32)

    xbuf[8:8 + tt, :] = xr_ref[...]
    row = lax.broadcasted_iota(jnp.int32, (tt, RNN_BW), 0)
    first_row = jnp.logical_and(row == 0, ti == 0)

    def block(n, carry):
        col = pl.ds(pl.multiple_of(n * RNN_BW, RNN_BW), RNN_BW)
        xc = cb_ref[:, col]
        for k in range(CONV_W):
            xc = xc + xbuf[8 - (CONV_W - 1) + k:8 - (CONV_W - 1) + k + tt, col] * cw_ref[k:k + 1, col]
        gg = _dot(xc.astype(BF16), wg_ref[n])
        a, b = _rglru_gates(xc, gg, bga_ref[:, col], bgx_ref[:, col], lam_ref[:, col], first_row)
        s = 1
        while s < tt:
            keep = row >= s
            a_sh = jnp.where(keep, pltpu.roll(a, s, 0), 1.0)
            b_sh = jnp.where(keep, pltpu.roll(b, s, 0), 0.0)
            b = a * b_sh + b
            a = a * a_sh
            s *= 2
        hs = a * hcar[:, col] + b
        hcar[:, col] = hs[tt - 1:tt, :]
        o_ref[:, col] = (hs * _gelu(yr_ref[:, col])).astype(o_ref.dtype)
        return carry

    lax.fori_loop(0, RNN_BLOCKS, block, 0, unroll=4)
    xbuf[0:8, :] = xbuf[tt:tt + 8, :]

    @pl.when(ti == pl.num_programs(1) - 1)
    def _():
        h_ref[0] = hcar[...]
        conv_ref[0] = xbuf[8 - (CONV_W - 1):8, :]


def _rglru_prompt(xy, conv_w, conv_b, wg, b_ga, b_gx, lam, nbatch, seq, tt):
    d = conv_w.shape[1]
    nt = seq // tt
    vec = pl.BlockSpec((1, d), lambda b, t: (0, 0))
    return pl.pallas_call(
        _rglru_prompt_kernel,
        grid=(nbatch, nt),
        in_specs=[pl.BlockSpec((tt, d), lambda b, t: (b * nt + t, 0)),
                  pl.BlockSpec((tt, d), lambda b, t: (b * nt + t, 1)),
                  pl.BlockSpec((CONV_W, d), lambda b, t: (0, 0)),
                  vec,
                  pl.BlockSpec((RNN_BLOCKS, RNN_BW, 2 * RNN_BW), lambda b, t: (0, 0, 0)),
                  vec, vec, vec],
        out_specs=[pl.BlockSpec((tt, d), lambda b, t: (b * nt + t, 0)),
                   pl.BlockSpec((1, 1, d), lambda b, t: (b, 0, 0)),
                   pl.BlockSpec((1, CONV_W - 1, d), lambda b, t: (b, 0, 0))],
        out_shape=[jax.ShapeDtypeStruct((nbatch * seq, d), BF16),
                   jax.ShapeDtypeStruct((nbatch, 1, d), F32),
                   jax.ShapeDtypeStruct((nbatch, CONV_W - 1, d), F32)],
        scratch_shapes=[pltpu.VMEM((tt + 8, d), F32), pltpu.VMEM((1, d), F32)],
        compiler_params=_cparams(("arbitrary", "arbitrary")),
        name="rglru_prompt",
    )(xy, xy, conv_w, conv_b, wg, b_ga, b_gx, lam)


def _rglru_sample_kernel(xr_ref, yr_ref, cst_ref, h0_ref, cw_ref, cb_ref, wg_ref, bga_ref, bgx_ref, lam_ref,
                         o_ref, h_ref, conv_ref, *, n_tok, pos0):
    nb = h0_ref.shape[0]
    xs = [cst_ref[k] for k in range(CONV_W - 1)] + [xr_ref[t * nb:(t + 1) * nb, :] for t in range(n_tok)]
    h = h0_ref[...]
    for t in range(n_tok):
        xc = cb_ref[...]
        for k in range(CONV_W):
            xc = xc + xs[t + k] * cw_ref[k:k + 1, :]
        gg = _dot(xc.astype(BF16), wg_ref[0])
        a, b = _rglru_gates(xc, gg, bga_ref[...], bgx_ref[...], lam_ref[...], True if pos0 + t == 0 else None)
        h = a * h + b
        o_ref[:, t, :] = h * _gelu(yr_ref[t * nb:(t + 1) * nb, :])
    h_ref[...] = h
    for k in range(CONV_W - 1):
        conv_ref[k] = xs[n_tok + k]


def _rglru_sample(xy, cst, h0, conv_w, conv_b, wg, b_ga, b_gx, lam, n_tok, pos0):
    nb, d = h0.shape
    m = nb * n_tok
    vec = pl.BlockSpec((1, RNN_BW), lambda n: (0, n))
    st = pl.BlockSpec((CONV_W - 1, nb, RNN_BW), lambda n: (0, 0, n))
    return pl.pallas_call(
        functools.partial(_rglru_sample_kernel, n_tok=n_tok, pos0=pos0),
        grid=(RNN_BLOCKS,),
        in_specs=[pl.BlockSpec((m, RNN_BW), lambda n: (0, n)),
                  pl.BlockSpec((m, RNN_BW), lambda n: (0, RNN_BLOCKS + n)),
                  st,
                  pl.BlockSpec((nb, RNN_BW), lambda n: (0, n)),
                  pl.BlockSpec((CONV_W, RNN_BW), lambda n: (0, n)),
                  vec,
                  pl.BlockSpec((1, RNN_BW, 2 * RNN_BW), lambda n: (n, 0, 0)),
                  vec, vec, vec],
        out_specs=[pl.BlockSpec((nb, n_tok, RNN_BW), lambda n: (0, 0, n)),
                   pl.BlockSpec((nb, RNN_BW), lambda n: (0, n)),
                   st],
        out_shape=[jax.ShapeDtypeStruct((nb, n_tok, d), F32),
                   jax.ShapeDtypeStruct((nb, d), F32),
                   jax.ShapeDtypeStruct((CONV_W - 1, nb, d), F32)],
        compiler_params=_cparams(("parallel",)),
        name="rglru_sample",
    )(xy, xy, cst, h0, conv_w, conv_b, wg, b_ga, b_gx, lam)


def _block_summary(x, pe_ref, w_ref, nblk):
    m = jnp.sum(x.reshape(nblk, CMP_BLK, HEAD_DIM), axis=1) * (1.0 / CMP_BLK)
    m = m + jnp.mean(pe_ref[...], axis=0, keepdims=True)
    return _dot(m.astype(BF16), w_ref[...])


def _cmp_prompt_kernel(kc_ref, vc_ref, pek_ref, pev_ref, wk_ref, wv_ref, cos_ref, sin_ref, ko_ref, vo_ref):
    nblk = ko_ref.shape[2]
    rows = nblk * CMP_BLK
    for g in range(N_KV):
        kk = _block_summary(kc_ref[pl.ds(g, rows, stride=N_KV), :], pek_ref, wk_ref, nblk)
        ko_ref[0, g] = _rope(kk, cos_ref[...], sin_ref[...]).astype(BF16)
        vo_ref[0, g] = _block_summary(vc_ref[pl.ds(g, rows, stride=N_KV), :], pev_ref, wv_ref, nblk).astype(BF16)


def _cmp_prompt(kc, vc, pe_k, pe_v, w_k, w_v, cos_e, sin_e, nbatch, seq, rows):
    nblk = rows // CMP_BLK
    nt = seq // rows
    nat = pl.BlockSpec((rows * N_KV, HEAD_DIM), lambda b, t: (b * nt + t, 0))
    full = lambda shape: pl.BlockSpec(shape, lambda b, t: (0,) * len(shape))
    tab = pl.BlockSpec((nblk, HEAD_DIM), lambda b, t: (t, 0))
    out = pl.BlockSpec((1, N_KV, nblk, HEAD_DIM), lambda b, t: (b, 0, t, 0))
    shape = jax.ShapeDtypeStruct((nbatch, N_KV, seq // CMP_BLK, HEAD_DIM), BF16)
    return pl.pallas_call(
        _cmp_prompt_kernel,
        grid=(nbatch, nt),
        in_specs=[nat, nat, full((CMP_BLK, HEAD_DIM)), full((CMP_BLK, HEAD_DIM)),
                  full((HEAD_DIM, HEAD_DIM)), full((HEAD_DIM, HEAD_DIM)), tab, tab],
        out_specs=[out, out],
        out_shape=[shape, shape],
        compiler_params=_cparams(("parallel", "arbitrary")),
        name="cmp_prompt",
    )(kc, vc, pe_k, pe_v, w_k, w_v, cos_e, sin_e)


def _page_copies(pt_ref, hbm, buf, sem, bb, slot, n_pages):
    pr = PAGE_SIZE * N_KV
    return [pltpu.make_async_copy(hbm.at[pl.ds(pl.multiple_of(pt_ref[bb, p] * pr, pr), pr)],
                                  buf.at[slot, pl.ds(p * pr, pr)], sem.at[slot])
            for p in range(n_pages)]


def _gather_step(pt_ref, hbms, bufs, sems, n_pages):
    b = pl.program_id(0)
    slot = b % 2

    @pl.when(b == 0)
    def _():
        for hbm, buf, sem in zip(hbms, bufs, sems):
            for c in _page_copies(pt_ref, hbm, buf, sem, 0, 0, n_pages):
                c.start()

    @pl.when(b + 1 < pl.num_programs(0))
    def _():
        for hbm, buf, sem in zip(hbms, bufs, sems):
            for c in _page_copies(pt_ref, hbm, buf, sem, b + 1, 1 - slot, n_pages):
                c.start()

    for hbm, buf, sem in zip(hbms, bufs, sems):
        for c in _page_copies(pt_ref, hbm, buf, sem, b, slot, n_pages):
            c.wait()
    return slot


def _cmp_sample_kernel(pt_ref, kc_hbm, vc_hbm, kn_ref, vn_ref, pek_ref, pev_ref, wk_ref, wv_ref, cos_ref, sin_ref,
                       ko_ref, vo_ref, kbuf, vbuf, ksem, vsem, *, n_pages):
    slot = _gather_step(pt_ref, (kc_hbm, vc_hbm), (kbuf, vbuf), (ksem, vsem), n_pages)
    past = n_pages * PAGE_SIZE
    nblk = past // CMP_BLK
    npad = ko_ref.shape[2] - nblk
    n_tok = kn_ref.shape[0] // N_KV

    def tail(new_ref, pe_ref, w_ref, g):
        m = (jnp.sum(new_ref[pl.ds(g, n_tok, stride=N_KV), :], axis=0, keepdims=True)
             + jnp.sum(pe_ref[...], axis=0, keepdims=True)) * (1.0 / CMP_BLK)
        return _dot(jnp.broadcast_to(m, (npad, HEAD_DIM)).astype(BF16), w_ref[...])

    for g in range(N_KV):
        kk = _block_summary(kbuf[slot, pl.ds(g, past, stride=N_KV), :], pek_ref, wk_ref, nblk)
        ko_ref[0, g, 0:nblk, :] = _rope(kk, cos_ref[0:nblk, :], sin_ref[0:nblk, :])
        kt = tail(kn_ref, pek_ref, wk_ref, g)
        ko_ref[0, g, nblk:nblk + npad, :] = _rope(kt, cos_ref[nblk:nblk + npad, :], sin_ref[nblk:nblk + npad, :])
        vo_ref[0, g, 0:nblk, :] = _block_summary(vbuf[slot, pl.ds(g, past, stride=N_KV), :], pev_ref, wv_ref, nblk)
        vo_ref[0, g, nblk:nblk + npad, :] = tail(vn_ref, pev_ref, wv_ref, g)


def _cmp_sample(page_table, cache_k, cache_v, k_new, v_new, pe_k, pe_v, w_k, w_v, cos_e, sin_e, npad):
    nb, n_pages = page_table.shape
    past = n_pages * PAGE_SIZE
    nblk = past // CMP_BLK
    n_tok = k_new.shape[0] // (nb * N_KV)
    full = lambda shape: pl.BlockSpec(shape, lambda b, pt: (0,) * len(shape))
    new = pl.BlockSpec((n_tok * N_KV, HEAD_DIM), lambda b, pt: (b, 0))
    out = pl.BlockSpec((1, N_KV, nblk + npad, HEAD_DIM), lambda b, pt: (b, 0, 0, 0))
    shape = jax.ShapeDtypeStruct((nb, N_KV, nblk + npad, HEAD_DIM), F32)
    return pl.pallas_call(
        functools.partial(_cmp_sample_kernel, n_pages=n_pages),
        grid_spec=pltpu.PrefetchScalarGridSpec(
            num_scalar_prefetch=1,
            grid=(nb,),
            in_specs=[pl.BlockSpec(memory_space=pl.ANY), pl.BlockSpec(memory_space=pl.ANY), new, new,
                      full((CMP_BLK, HEAD_DIM)), full((CMP_BLK, HEAD_DIM)),
                      full((HEAD_DIM, HEAD_DIM)), full((HEAD_DIM, HEAD_DIM)),
                      full((nblk + npad, HEAD_DIM)), full((nblk + npad, HEAD_DIM))],
            out_specs=[out, out],
            scratch_shapes=[pltpu.VMEM((2, past * N_KV, HEAD_DIM), F32), pltpu.VMEM((2, past * N_KV, HEAD_DIM), F32),
                            pltpu.SemaphoreType.DMA((2,)), pltpu.SemaphoreType.DMA((2,))]),
        out_shape=[shape, shape],
        compiler_params=_cparams(("arbitrary",)),
        name="cmp_sample",
    )(page_table, cache_k, cache_v, k_new, v_new, pe_k, pe_v, w_k, w_v, cos_e, sin_e)


def _select_blocks_t(imp_t, cur, nsel):
    nblk = imp_t.shape[0]
    nidx = lax.broadcasted_iota(jnp.int32, imp_t.shape, 0)
    forced = jnp.logical_or(jnp.logical_or(nidx == cur, nidx == cur - 1), nidx == 0)
    imp_t = jnp.where(forced, 1e4, imp_t)
    imp_t = jnp.where(nidx > cur, -1.0, imp_t)
    rank = jnp.zeros(imp_t.shape, F32)
    for i in range(nblk):
        row = imp_t[i:i + 1, :]
        rank = rank + jnp.where(nidx > i, jnp.where(row >= imp_t, 1.0, 0.0), jnp.where(row > imp_t, 1.0, 0.0))
    return jnp.where(rank < float(nsel), 1.0, 0.0)


def _softmax_lanes(s):
    m = jnp.max(s, axis=-1, keepdims=True)
    e = jnp.exp2(s - m)
    return e / jnp.sum(e, axis=-1, keepdims=True)


def _attn_prompt_kernel(q_ref, kc_ref, vc_ref, ks_ref, vs_ref, kw_ref, vw_ref, e_ref, gate_ref, o_ref, part,
                        *, kt, kv_unroll, nblk, nsel):
    qi = pl.program_id(2)
    tq = q_ref.shape[1]
    nblk_pad = kc_ref.shape[2]
    hh = GQA_R
    nch = GQA_R // hh
    rows = hh * tq
    qs = [q_ref[c * hh:(c + 1) * hh].reshape(rows, HEAD_DIM) for c in range(nch)]
    qpos = qi * tq + lax.broadcasted_iota(jnp.int32, (tq, 1), 0)
    any_valid = jnp.where(qpos >= CMP_BLK - 1, 1.0, 0.0)

    kc = kc_ref[0, 0]
    blk_end = lax.broadcasted_iota(jnp.int32, (1, nblk_pad), 1) * CMP_BLK + (CMP_BLK - 1)
    valid = jnp.logical_and(blk_end <= qpos, blk_end < nblk * CMP_BLK)
    o_cmp = []
    imp = jnp.zeros((tq, nblk_pad), F32)
    for c in range(nch):
        s = jnp.where(valid[None], _dot_nt(qs[c], kc).reshape(hh, tq, nblk_pad), NEG)
        e = jnp.exp2(s - jnp.max(s, axis=-1, keepdims=True))
        inv = any_valid[None] / jnp.sum(e, axis=-1, keepdims=True)
        o_cmp.append(_dot(e.reshape(rows, nblk_pad).astype(BF16), vc_ref[0, 0]) * inv.reshape(rows, 1))
        imp = imp + jnp.sum(e * inv, axis=0)

    qpos_l = qi * tq + lax.broadcasted_iota(jnp.int32, (1, tq), 1)
    sel_t = _select_blocks_t(imp.T[:nblk], qpos_l // CMP_BLK, nsel)
    pad = e_ref.shape[0] - nblk
    if pad:
        sel_t = jnp.concatenate([sel_t, jnp.zeros((pad, tq), F32)], axis=0)
    sel = sel_t.T.astype(BF16)

    def gated(branch, outs):
        gate = gate_ref[0]
        return [gate[:, 3 * (c * hh + r) + branch:3 * (c * hh + r) + branch + 1] * outs[c][r * tq:(r + 1) * tq]
                for c in range(nch) for r in range(hh)]

    span = WINDOW + tq
    start = pl.multiple_of(jnp.maximum(qi * tq - WINDOW, 0), tq)
    kpos_w = start + lax.broadcasted_iota(jnp.int32, (1, span), 1)
    band = jnp.logical_and(kpos_w <= qpos, kpos_w > qpos - WINDOW)
    k_w = kw_ref[0, pl.ds(start, span), :]
    v_w = vw_ref[0, pl.ds(start, span), :]
    o_win = []
    for c in range(nch):
        sw = jnp.where(band[None], _dot_nt(qs[c], k_w).reshape(hh, tq, span), NEG)
        ew = jnp.exp2(sw - jnp.max(sw, axis=-1, keepdims=True))
        inv_w = 1.0 / jnp.sum(ew, axis=-1, keepdims=True)
        o_win.append(_dot(ew.reshape(rows, span).astype(BF16), v_w) * inv_w.reshape(rows, 1))
    for h, (a, b) in enumerate(zip(gated(0, o_cmp), gated(2, o_win))):
        part[:, h * HEAD_DIM:(h + 1) * HEAD_DIM] = a + b

    def kv_tile(j, carry):
        off = pl.multiple_of(j * kt, kt)
        kpos = off + lax.broadcasted_iota(jnp.int32, (1, kt), 1)
        hit = _dot(sel, e_ref[:, pl.ds(off, kt)])
        bias = jnp.where(kpos <= qpos, (hit - 1.0) * (-NEG), NEG)
        k_t = ks_ref[0, pl.ds(off, kt), :]
        v_t = vs_ref[0, pl.ds(off, kt), :]
        out = []
        for c in range(nch):
            m_i, l_i, acc = carry[3 * c:3 * c + 3]
            sj = _dot_nt(qs[c], k_t).reshape(hh, tq, kt) + bias[None]
            m_new = jnp.maximum(m_i, jnp.max(sj, axis=-1, keepdims=True))
            alpha = jnp.exp2(m_i - m_new)
            pj = jnp.exp2(sj - m_new)
            l_new = alpha * l_i + jnp.sum(pj, axis=-1, keepdims=True)
            pv = _dot(pj.reshape(rows, kt).astype(BF16), v_t)
            out += [m_new, l_new, acc * alpha.reshape(rows, 1) + pv]
        return tuple(out)

    def kv_step(jj, carry):
        for u in range(kv_unroll):
            carry = kv_tile(jj * kv_unroll + u, carry)
        return carry

    n_tiles = (qi * tq + tq + kt - 1) // kt
    init = (jnp.full((hh, tq, 1), NEG, F32), jnp.zeros((hh, tq, 1), F32), jnp.zeros((rows, HEAD_DIM), F32)) * nch
    fin = lax.fori_loop(0, n_tiles // kv_unroll, kv_step, init)
    for u in range(kv_unroll - 1):
        j_left = (n_tiles // kv_unroll) * kv_unroll + u
        fin = lax.cond(j_left < n_tiles, functools.partial(kv_tile, j_left), lambda c: c, fin)
    o_sel = [fin[3 * c + 2] / fin[3 * c + 1].reshape(rows, 1) for c in range(nch)]

    for h, a in enumerate(gated(1, o_sel)):
        cs = slice(h * HEAD_DIM, (h + 1) * HEAD_DIM)
        o_ref[:, cs] = (part[:, cs] + a).astype(BF16)


def _attn_prompt(q_hm, kcmp, vcmp, kv_hm, expand, gates_grp, nbatch, seq, tq, kt):
    nq = seq // tq
    nblk = seq // CMP_BLK
    nblk_pad = expand.shape[0]
    kcmp, vcmp = (jnp.pad(a, ((0, 0), (0, 0), (0, nblk_pad - nblk), (0, 0))) for a in (kcmp, vcmp))
    kv = pl.BlockSpec((1, seq, HEAD_DIM), lambda b, g, i: (g, b, 0))
    cmp = pl.BlockSpec((1, 1, nblk_pad, HEAD_DIM), lambda b, g, i: (b, g, 0, 0))
    return pl.pallas_call(
        functools.partial(_attn_prompt_kernel, kt=kt, kv_unroll=2, nblk=nblk, nsel=min(N_SEL, nblk)),
        grid=(nbatch, N_KV, nq),
        in_specs=[pl.BlockSpec((GQA_R, tq, HEAD_DIM), lambda b, g, i: (g, b * nq + i, 0)),
                  cmp, cmp, kv, kv, kv, kv,
                  pl.BlockSpec(expand.shape, lambda b, g, i: (0, 0)),
                  pl.BlockSpec((1, tq, LANES), lambda b, g, i: (g, b * nq + i, 0))],
        out_specs=pl.BlockSpec((tq, GQA_R * HEAD_DIM), lambda b, g, i: (b * nq + i, g)),
        out_shape=jax.ShapeDtypeStruct((nbatch * seq, D_ATT), BF16),
        scratch_shapes=[pltpu.VMEM((tq, GQA_R * HEAD_DIM), F32)],
        compiler_params=_cparams(("parallel", "parallel", "arbitrary")),
        name="attn_prompt",
    )(q_hm, kcmp, vcmp, *kv_hm, expand, gates_grp)


def _attn_sample_kernel(pt_ref, ks_hbm, vs_hbm, q_ref, kc_ref, vc_ref, ksn_ref, vsn_ref, kwc_ref, vwc_ref,
                        kwn_ref, vwn_ref, e_ref, o_ref, okw_ref, ovw_ref, kbuf, vbuf, newbuf, ksem, vsem,
                        *, n_pages, n_tok, nsel):
    b = pl.program_id(0)
    slot = _gather_step(pt_ref, (ks_hbm, vs_hbm), (kbuf, vbuf), (ksem, vsem), n_pages)
    past = n_pages * PAGE_SIZE
    nblk_c = past // CMP_BLK
    nblk = kc_ref.shape[2]
    rows = GQA_R * n_tok
    lw = kwc_ref.shape[0] // N_KV
    nr = n_tok * N_KV

    @pl.when(b == 0)
    def _():
        newbuf[...] = jnp.zeros(newbuf.shape, F32)

    for i, ref in enumerate((ksn_ref, vsn_ref, kwn_ref, vwn_ref)):
        newbuf[i, 0:nr, :] = ref[...]

    okw_ref[0:(lw - n_tok) * N_KV, :] = kwc_ref[nr:lw * N_KV, :]
    okw_ref[(lw - n_tok) * N_KV:lw * N_KV, :] = kwn_ref[...]
    ovw_ref[0:(lw - n_tok) * N_KV, :] = vwc_ref[nr:lw * N_KV, :]
    ovw_ref[(lw - n_tok) * N_KV:lw * N_KV, :] = vwn_ref[...]

    rows_all = N_KV * rows
    t_row = lax.broadcasted_iota(jnp.int32, (rows_all, 1), 0) % n_tok
    qpos = past + t_row
    nidx = lax.broadcasted_iota(jnp.int32, (rows_all, nblk), 1)
    valid = nidx * CMP_BLK + (CMP_BLK - 1) <= qpos
    cur = qpos // CMP_BLK
    forced = jnp.logical_or(jnp.logical_or(nidx == cur, nidx == cur - 1), nidx == 0)
    new_lane = lax.broadcasted_iota(jnp.int32, (1, PAGE_SIZE), 1)
    new_ok = jnp.logical_and(new_lane < n_tok, past + new_lane <= qpos)
    wpos = (past - lw) + lax.broadcasted_iota(jnp.int32, (1, lw), 1)
    win_ok = jnp.logical_and(wpos <= qpos, wpos > qpos - WINDOW)
    win_new_ok = jnp.logical_and(new_ok, past + new_lane > qpos - WINDOW)

    groups = range(N_KV)
    grp = lambda x, g: x[g * rows:(g + 1) * rows]
    stack = lambda f: jnp.concatenate([f(g) for g in groups], axis=0)
    group_rows = lambda ref, n: (lambda g: ref[pl.ds(g, n, stride=N_KV), :].astype(BF16))
    new_rows = lambda i: (lambda g: newbuf[i, pl.ds(g, PAGE_SIZE, stride=N_KV), :].astype(BF16))
    qs = [q_ref[g, 0].astype(BF16) for g in groups]

    def attend(s_parts, v_parts):
        ps = _softmax_lanes(jnp.concatenate(s_parts, axis=1)).astype(BF16)
        outs = []
        for g in groups:
            pg, off, acc = grp(ps, g), 0, None
            for sp, vp in zip(s_parts, v_parts):
                d = _dot(pg[:, off:off + sp.shape[1]], vp(g))
                acc = d if acc is None else acc + d
                off += sp.shape[1]
            outs.append(acc)
        return outs

    p = _softmax_lanes(jnp.where(valid, stack(lambda g: _dot_nt(qs[g], kc_ref[0, g].astype(BF16))), NEG))
    p = p * jnp.where(qpos >= CMP_BLK - 1, 1.0, 0.0)
    o_cmp = [_dot(grp(p, g).astype(BF16), vc_ref[0, g].astype(BF16)) for g in groups]

    def head_sum(pg):
        u = pg + pltpu.roll(pg, (GQA_R // 2) * n_tok, 0)
        return u + pltpu.roll(u, n_tok, 0)

    imp = stack(lambda g: head_sum(grp(p, g)))
    imp = jnp.where(forced, 1e4, imp)
    imp = jnp.where(nidx > cur, -1.0, imp)
    rank = jnp.zeros((rows_all, nblk), F32)
    for i in range(nblk):
        col = imp[:, i:i + 1]
        rank = rank + jnp.where(nidx > i, jnp.where(col >= imp, 1.0, 0.0), jnp.where(col > imp, 1.0, 0.0))
    sel = jnp.where(rank < float(nsel), 1.0, 0.0)

    hit = _dot(sel.astype(BF16), e_ref[...])
    k_cached = lambda g: kbuf[slot, pl.ds(g, past, stride=N_KV), :].astype(BF16)
    v_cached = lambda g: vbuf[slot, pl.ds(g, past, stride=N_KV), :].astype(BF16)
    s_c = jnp.where(hit > 0.5, stack(lambda g: _dot_nt(qs[g], k_cached(g))), NEG)
    new_sel = jnp.logical_and(new_ok, sel[:, nblk_c:nblk_c + 1] > 0.5)
    s_n = jnp.where(new_sel, stack(lambda g: _dot_nt(qs[g], new_rows(0)(g))), NEG)
    o_sel = attend([s_c, s_n], [v_cached, new_rows(1)])

    s_w = jnp.where(win_ok, stack(lambda g: _dot_nt(qs[g], group_rows(kwc_ref, lw)(g))), NEG)
    s_wn = jnp.where(win_new_ok, stack(lambda g: _dot_nt(qs[g], new_rows(2)(g))), NEG)
    o_win = attend([s_w, s_wn], [group_rows(vwc_ref, lw), new_rows(3)])

    for j, branch in enumerate((o_cmp, o_sel, o_win)):
        for g in groups:
            for r in range(GQA_R):
                h = g * GQA_R + r
                for t in range(n_tok):
                    o_ref[j, 0, t:t + 1, h * HEAD_DIM:(h + 1) * HEAD_DIM] = branch[g][r * n_tok + t:r * n_tok + t + 1, :]


def _attn_sample(page_table, cache_ks, cache_vs, q_s, kcmp, vcmp, ks_new, vs_new, cache_kw, cache_vw, kw_new, vw_new,
                 expand):
    nb, n_pages = page_table.shape
    past = n_pages * PAGE_SIZE
    n_tok = ks_new.shape[0] // (nb * N_KV)
    rows = GQA_R * n_tok
    nblk = kcmp.shape[2]
    lw = cache_kw.shape[0] // (nb * N_KV)
    new = pl.BlockSpec((n_tok * N_KV, HEAD_DIM), lambda b, pt: (b, 0))
    cmp = pl.BlockSpec((1, N_KV, nblk, HEAD_DIM), lambda b, pt: (b, 0, 0, 0))
    win = pl.BlockSpec((lw * N_KV, HEAD_DIM), lambda b, pt: (b, 0))
    any_ = pl.BlockSpec(memory_space=pl.ANY)
    return pl.pallas_call(
        functools.partial(_attn_sample_kernel, n_pages=n_pages, n_tok=n_tok, nsel=min(N_SEL, past // CMP_BLK + 1)),
        grid_spec=pltpu.PrefetchScalarGridSpec(
            num_scalar_prefetch=1,
            grid=(nb,),
            in_specs=[any_, any_,
                      pl.BlockSpec((N_KV, 1, rows, HEAD_DIM), lambda b, pt: (0, b, 0, 0)),
                      cmp, cmp, new, new, win, win, new, new,
                      pl.BlockSpec(expand.shape, lambda b, pt: (0, 0))],
            out_specs=[pl.BlockSpec((3, 1, n_tok, D_ATT), lambda b, pt: (0, b, 0, 0)), win, win],
            scratch_shapes=[pltpu.VMEM((2, past * N_KV, HEAD_DIM), F32), pltpu.VMEM((2, past * N_KV, HEAD_DIM), F32),
                            pltpu.VMEM((4, PAGE_SIZE * N_KV, HEAD_DIM), F32),
                            pltpu.SemaphoreType.DMA((2,)), pltpu.SemaphoreType.DMA((2,))]),
        out_shape=[jax.ShapeDtypeStruct((3, nb, n_tok, D_ATT), F32),
                   jax.ShapeDtypeStruct(cache_kw.shape, F32), jax.ShapeDtypeStruct(cache_vw.shape, F32)],
        compiler_params=_cparams(("arbitrary",)),
        name="attn_sample",
    )(page_table, cache_ks, cache_vs, q_s, kcmp, vcmp, ks_new, vs_new, cache_kw, cache_vw, kw_new, vw_new, expand)


def _expand_matrix(rows, nkeys):
    return (jnp.arange(rows, dtype=jnp.int32)[:, None] == (jnp.arange(nkeys, dtype=jnp.int32) // CMP_BLK)[None, :]).astype(BF16)


def _pick(m, pref):
    while m % pref:
        pref //= 2
    return pref


def kernel(x_prompt, x_sample, cache_k_cmp, cache_v_cmp, cache_k_sel, cache_v_sel, cache_k_win, cache_v_win, state_h, state_conv, page_table, norm_mix_pre, w_in, conv_w, conv_b, w_ga, b_ga, w_gx, b_gx, lam, pe_k, pe_v, w_phi_k, w_phi_v, norm_rnn_out, norm_att_out, w_out, norm_mix_post, norm_mlp_pre, w_up, w_down, norm_mlp_post):
    depth = w_in.shape[0]
    assert depth == 1, "single layer only"
    bp, seq, d_model = x_prompt.shape
    bs, n_tok, _ = x_sample.shape
    d_rnn = conv_w.shape[-1]
    n_pages = page_table.shape[1]
    past = n_pages * PAGE_SIZE
    lw_in = cache_k_win.shape[2]
    assert lw_in == WINDOW and past % CMP_BLK == 0 and n_tok <= CMP_BLK and d_rnn == RNN_BLOCKS * RNN_BW
    l = 0

    w = w_in[l].astype(BF16)
    c0 = 2 * d_rnn
    q_blk0 = c0 // (GQA_R * HEAD_DIM)
    kv_blk0 = (c0 + D_ATT) // D_KV
    w_gl = jnp.pad(w[:, c0 + D_ATT + 6 * D_KV:], ((0, 0), (0, LANES - 3 * N_HEADS)))
    w_o = w_out[l].astype(BF16)
    w_u = w_up[l].astype(BF16)
    w_d = w_down[l].astype(BF16)
    wg = jnp.concatenate([w_ga[l], w_gx[l]], axis=-1).astype(BF16)
    wk_phi = w_phi_k[l].astype(BF16)
    wv_phi = w_phi_v[l].astype(BF16)
    vec = lambda a: a[l].reshape(1, -1)
    rnn_p = (conv_w[l], vec(conv_b), wg, vec(b_ga), vec(b_gx), vec(lam))

    def mixer_tail(x2d, cat):
        tm_t = _pick(x2d.shape[0], 512)
        h, hn = _out_proj(cat, w_o, x2d, vec(norm_mix_post), vec(norm_mlp_pre), _pick(x2d.shape[0], 256), w_o.shape[1])
        return _mlp(hn, w_u, w_d, h, vec(norm_mlp_post), tm_t, 512)

    mp = bp * seq
    xp2 = x_prompt.reshape(mp, d_model)
    tm_p = _pick(seq, 1024)
    cos_p, sin_p = _rope_tables(jnp.arange(seq, dtype=jnp.int32))
    xn = _norm_cast(xp2, vec(norm_mix_pre), _pick(mp, 512))
    xy = _matmul(xn, w, tm_p, 512, name="proj_xy", n=c0)
    q_hm = _proj_q_prompt(xn, w, q_blk0, cos_p, sin_p, tm_p)
    kv_p = [_proj_kv(xn, w, kv_blk0 + s, cos_p, sin_p, tm_p, KV_ROPE[s], s >= 2) for s in range(6)]
    kv_nat = [o[0] for o in kv_p]
    kv_hm = [o[1] for o in kv_p[2:]]
    gates = _matmul(xn, w_gl, tm_p, LANES, body=_mm_gate_kernel, name="proj_gate")
    o_rnn, h_p, cv_p = _rglru_prompt(xy, *rnn_p, bp, seq, _pick(seq, 128))
    nblk_p = seq // CMP_BLK
    cos_e, sin_e = _rope_tables(jnp.arange(nblk_p, dtype=jnp.int32) * CMP_BLK + CMP_BLK - 1)
    cmp_rows = _pick(seq, 1024)
    kcmp, vcmp = _cmp_prompt(kv_nat[0], kv_nat[1], pe_k[l], pe_v[l], wk_phi, wv_phi, cos_e, sin_e, bp, seq, cmp_rows)
    e_rows = max(LANES, nblk_p)
    gates_grp = jnp.pad(gates[:, :3 * N_HEADS].reshape(mp, N_KV, 3 * GQA_R).swapaxes(0, 1),
                        ((0, 0), (0, 0), (0, LANES - 3 * GQA_R)))
    o_att = _attn_prompt(q_hm, kcmp, vcmp, kv_hm, _expand_matrix(e_rows, seq), gates_grp, bp, seq, _pick(seq, 256),
                         _pick(seq, 512))
    cat_p = _pair_norm(o_rnn, o_att, vec(norm_rnn_out), vec(norm_att_out), _pick(mp, 512))
    y_p = mixer_tail(xp2, cat_p).reshape(bp, seq, d_model)
    lw_p = min(WINDOW, seq)
    nat5 = lambda a: a.reshape(1, bp, seq, N_KV, HEAD_DIM)
    p_out = (nat5(kv_nat[0]), nat5(kv_nat[1]), nat5(kv_nat[2]), nat5(kv_nat[3]),
             nat5(kv_nat[4])[:, :, seq - lw_p:], nat5(kv_nat[5])[:, :, seq - lw_p:],
             h_p.reshape(1, bp, d_rnn), cv_p.reshape(1, bp, CONV_W - 1, d_rnn))

    ms = bs * n_tok
    xs_tb = x_sample.swapaxes(0, 1).reshape(ms, d_model)
    pos_s = past + jnp.arange(n_tok, dtype=jnp.int32)
    cos_s, sin_s = _rope_tables(jnp.repeat(pos_s, bs))
    xn_s = _norm_cast(xs_tb, vec(norm_mix_pre), ms)
    xy_s = _matmul(xn_s, w, ms, 512, name="proj_xy", n=c0)
    q_s = _proj_q_sample(xn_s, w, q_blk0, cos_s, sin_s, n_tok)
    kv_s = [_proj_kv(xn_s, w, kv_blk0 + s, cos_s, sin_s, ms, KV_ROPE[s], False)[0] for s in range(6)]
    gates_s = _matmul(xn_s, w_gl, ms, LANES, body=_mm_gate_kernel, name="proj_gate")
    o_rnn_s, h_s, cv_s = _rglru_sample(xy_s, state_conv[l].swapaxes(0, 1), state_h[l], *rnn_p, n_tok, past)
    kv_bt = [a.reshape(n_tok, bs, N_KV, HEAD_DIM).swapaxes(0, 1) for a in kv_s]
    kv_s2 = [a.reshape(ms * N_KV, HEAD_DIM) for a in kv_bt]
    rows2d = lambda c: c.reshape(-1, HEAD_DIM)
    nblk_c = past // CMP_BLK
    npad = 8
    cos_es, sin_es = _rope_tables(jnp.arange(nblk_c + npad, dtype=jnp.int32) * CMP_BLK + CMP_BLK - 1)
    kcmp_s, vcmp_s = _cmp_sample(page_table, rows2d(cache_k_cmp), rows2d(cache_v_cmp), kv_s2[0], kv_s2[1],
                                 pe_k[l], pe_v[l], wk_phi, wv_phi, cos_es, sin_es, npad)
    o3_s, kw_out, vw_out = _attn_sample(page_table, rows2d(cache_k_sel), rows2d(cache_v_sel), q_s, kcmp_s, vcmp_s,
                                        kv_s2[2], kv_s2[3], rows2d(cache_k_win), rows2d(cache_v_win), kv_s2[4], kv_s2[5],
                                        _expand_matrix(nblk_c + npad, past))
    gates_bt = gates_s.reshape(n_tok, bs, LANES).swapaxes(0, 1).reshape(ms, LANES)
    cat_s = _cat_norm(o_rnn_s.reshape(ms, d_rnn), o3_s.reshape(3, ms, D_ATT), gates_bt, vec(norm_rnn_out),
                      vec(norm_att_out), _pick(ms, 256))
    y_s = mixer_tail(x_sample.reshape(ms, d_model), cat_s).reshape(bs, n_tok, d_model)
    new5 = lambda a: a.reshape(1, bs, n_tok, N_KV, HEAD_DIM)
    s_out = (new5(kv_bt[0]), new5(kv_bt[1]), new5(kv_bt[2]), new5(kv_bt[3]),
             kw_out.reshape(cache_k_win.shape), vw_out.reshape(cache_v_win.shape),
             h_s.reshape(1, bs, d_rnn), cv_s.swapaxes(0, 1).reshape(1, bs, CONV_W - 1, d_rnn))
    return (y_p, y_s) + p_out + s_out
```
